```python
import math
import jax, jax.numpy as jnp
from jax import lax
import numpy as np

D_MODEL = 1024
BATCH = 8
SEQ = 2048
DEPTH = 1

SB_WIDTH = D_MODEL // 2
SB_HEAD_DIM = 64
SB_HEADS = SB_WIDTH // SB_HEAD_DIM
SB_BLOCK = 128
HGRN_WIDTH = D_MODEL // 2
HGRN_HEAD_DIM = 128
HGRN_HEADS = HGRN_WIDTH // HGRN_HEAD_DIM
HGRN_CHUNK = 64
N_EXPERTS = 32
TOP_K = 4
D_FF = D_MODEL
SWIGLU_ALPHA = 1.702
SWIGLU_LIMIT = 7.0
EXPERT_BLOCK = 128
DEEPNORM_ALPHA = (2 * DEPTH) ** 0.25
DEEPNORM_BETA = (8 * DEPTH) ** -0.25
LN_EPS = 1e-5
RMS_EPS = 1e-6
IN_SPLITS = (SB_WIDTH, SB_WIDTH, SB_WIDTH, HGRN_WIDTH, HGRN_WIDTH, HGRN_WIDTH, HGRN_WIDTH, D_MODEL, D_MODEL)
IN_WIDTH = sum(IN_SPLITS)

kernel_name = 'hybrid_stickbreak_hgrn2_moe_deepnorm'


def layer_norm(x, g, b):
    xf = x.astype(jnp.float32)
    mu = jnp.mean(xf, axis=-1, keepdims=True)
    var = jnp.mean(jnp.square(xf - mu), axis=-1, keepdims=True)
    return ((xf - mu) * lax.rsqrt(var + LN_EPS) * g + b).astype(x.dtype)


def stick_breaking_attention(q, k, v):
    T = q.shape[2]
    scale = SB_HEAD_DIM ** -0.5
    qf, kf, vf = q.astype(jnp.float32), k.astype(jnp.float32), v.astype(jnp.float32)
    outs = []
    for blk in range(T // SB_BLOCK):
        t0 = blk * SB_BLOCK
        t1 = t0 + SB_BLOCK
        z = jnp.einsum('bhtd,bhsd->bhts', qf[:, :, t0:t1], kf[:, :, :t1]) * scale
        mask = jnp.arange(t1)[None, :] < (t0 + jnp.arange(SB_BLOCK))[:, None]
        log_keep = jnp.where(mask, jax.nn.log_sigmoid(-z), 0.0)
        log_after = lax.cumsum(log_keep, axis=3, reverse=True) - log_keep
        weights = jnp.where(mask, jnp.exp(jax.nn.log_sigmoid(z) + log_after), 0.0)
        outs.append(jnp.einsum('bhts,bhsd->bhtd', weights, vf[:, :, :t1]))
    return jnp.concatenate(outs, axis=2).astype(q.dtype)


def hgrn2_recurrence(q, k, logf, v):
    B, T, H, dk = q.shape
    dv = v.shape[-1]
    n_chunks = T // HGRN_CHUNK

    def to_chunks(a):
        return a.reshape(B, n_chunks, HGRN_CHUNK, H, a.shape[-1]).transpose(1, 0, 3, 2, 4)

    causal = jnp.tril(jnp.ones((HGRN_CHUNK, HGRN_CHUNK), dtype=bool))

    def step(S, xs):
        qc, kc, gc, vc = xs
        b = jnp.cumsum(gc, axis=2)
        o_inter = jnp.einsum('bhtk,bhkv->bhtv', qc * jnp.exp(b), S)
        diff = b[:, :, :, None, :] - b[:, :, None, :, :]
        decay = jnp.exp(jnp.where(causal[:, :, None], diff, -jnp.inf))
        scores = jnp.einsum('bhtsk,bhsk->bhts', qc[:, :, :, None, :] * decay, kc)
        o = o_inter + jnp.einsum('bhts,bhsv->bhtv', scores, vc)
        b_last = b[:, :, -1:, :]
        S_new = jnp.exp(b_last[:, :, 0, :])[..., None] * S + jnp.einsum(
            'bhsk,bhsv->bhkv', kc * jnp.exp(b_last - b), vc)
        return S_new, o

    S0 = jnp.zeros((B, H, dk, dv), jnp.float32)
    _, o = lax.scan(step, S0, (to_chunks(q), to_chunks(k), to_chunks(logf), to_chunks(v)))
    return o.transpose(1, 0, 3, 2, 4).reshape(B, T, H, dv)


def hybrid_mixer(x, w_in, lower_bound, hgrn_norm_g, w_branch_sb, w_branch_hgrn, w_out):
    B, T, _ = x.shape
    proj = x @ w_in
    offsets = np.cumsum(IN_SPLITS)[:-1].tolist()
    sq, sk, sv, hq, hf, hi, hg, gate_sb, gate_hg = jnp.split(proj, offsets, axis=-1)

    def heads(a):
        return a.reshape(B, T, SB_HEADS, SB_HEAD_DIM).transpose(0, 2, 1, 3)
    o_sb = stick_breaking_attention(heads(sq), heads(sk), heads(sv))
    o_sb = o_sb.transpose(0, 2, 1, 3).reshape(B, T, SB_WIDTH)

    f = lower_bound + (1.0 - lower_bound) * jax.nn.sigmoid(hf.astype(jnp.float32))
    shp = (B, T, HGRN_HEADS, HGRN_HEAD_DIM)
    o_hg = hgrn2_recurrence(jax.nn.silu(hq.astype(jnp.float32)).reshape(shp),
                            (1.0 - f).reshape(shp),
                            jnp.log(f).reshape(shp),
                            hi.astype(jnp.float32).reshape(shp))
    o_hg = o_hg * lax.rsqrt(jnp.mean(jnp.square(o_hg), axis=-1, keepdims=True) + RMS_EPS)
    o_hg = (o_hg.reshape(B, T, HGRN_WIDTH) * hgrn_norm_g * jax.nn.sigmoid(hg.astype(jnp.float32))).astype(x.dtype)

    merged = jax.nn.sigmoid(gate_sb) * (o_sb @ w_branch_sb) + jax.nn.sigmoid(gate_hg) * (o_hg @ w_branch_hgrn)
    return merged @ w_out


def moe(h, router_w, router_b, w_up, b_up, w_down, b_down):
    B, T, D = h.shape
    N = B * T
    xt = h.reshape(N, D)
    logits = (xt @ router_w + router_b).astype(jnp.float32)
    top_vals, top_idx = lax.top_k(logits, TOP_K)
    gates = jax.nn.softmax(top_vals, axis=-1)

    e_flat = top_idx.reshape(-1)
    tok_flat = jnp.arange(N * TOP_K, dtype=jnp.int32) // TOP_K
    order = jnp.argsort(e_flat)
    e_sorted = e_flat[order]
    tok_sorted = tok_flat[order]
    gate_sorted = gates.reshape(-1)[order].astype(h.dtype)

    counts = jnp.bincount(e_flat, length=N_EXPERTS)
    padded = (counts + EXPERT_BLOCK - 1) // EXPERT_BLOCK * EXPERT_BLOCK
    start = jnp.cumsum(counts) - counts
    pad_end = jnp.cumsum(padded)
    pad_start = pad_end - padded
    dest = pad_start[e_sorted] + (jnp.arange(N * TOP_K) - start[e_sorted])

    cap = N * TOP_K + N_EXPERTS * (EXPERT_BLOCK - 1)
    n_blocks = -(-cap // EXPERT_BLOCK)
    n_slots = n_blocks * EXPERT_BLOCK
    slot_tok = jnp.full((n_slots,), N, dtype=jnp.int32).at[dest].set(tok_sorted)
    x_pad = jnp.concatenate([xt, jnp.zeros((1, D), xt.dtype)], axis=0)
    xb = x_pad[slot_tok].reshape(n_blocks, EXPERT_BLOCK, D)
    block_e = jnp.minimum(jnp.searchsorted(pad_end, jnp.arange(n_blocks) * EXPERT_BLOCK, side='right'),
                          N_EXPERTS - 1)

    def expert_block(args):
        xblk, e = args
        hcat = xblk @ w_up[e] + b_up[e]
        x_glu = jnp.minimum(hcat[:, 0::2], SWIGLU_LIMIT)
        x_lin = jnp.clip(hcat[:, 1::2], -SWIGLU_LIMIT, SWIGLU_LIMIT)
        act = x_glu * jax.nn.sigmoid(SWIGLU_ALPHA * x_glu) * (x_lin + 1.0)
        return act @ w_down[e] + b_down[e]

    y_slots = lax.map(expert_block, (xb, block_e)).reshape(n_slots, D)
    contrib = y_slots[dest] * gate_sorted[:, None]
    out = jax.ops.segment_sum(contrib, tok_sorted, num_segments=N)
    return out.reshape(B, T, D).astype(h.dtype)


def setup_inputs(seed: int = 0) -> dict:
    key = jax.random.key(seed)
    ks = jax.random.split(key, 18)
    nrm = jax.random.normal
    beta = DEEPNORM_BETA
    col_scale = jnp.concatenate([jnp.full((w,), s, jnp.float32) for w, s in
                                 zip(IN_SPLITS, (1.0, 1.0, beta, 1.0, 1.0, beta, 1.0, 1.0, 1.0))])
    x = nrm(ks[0], (BATCH, SEQ, D_MODEL), jnp.float32)
    w_in = nrm(ks[1], (DEPTH, D_MODEL, IN_WIDTH), jnp.float32) * D_MODEL ** -0.5 * col_scale
    hgrn_lb_logits = 1.0 + 0.1 * nrm(ks[2], (DEPTH + 1, HGRN_WIDTH), jnp.float32)
    hgrn_norm_g = 1.0 + 0.05 * nrm(ks[3], (DEPTH, HGRN_WIDTH), jnp.float32)
    w_branch_sb = nrm(ks[4], (DEPTH, SB_WIDTH, D_MODEL), jnp.float32) * SB_WIDTH ** -0.5
    w_branch_hgrn = nrm(ks[5], (DEPTH, HGRN_WIDTH, D_MODEL), jnp.float32) * HGRN_WIDTH ** -0.5
    w_out = nrm(ks[6], (DEPTH, D_MODEL, D_MODEL), jnp.float32) * D_MODEL ** -0.5 * beta
    ln1_g = 1.0 + 0.05 * nrm(ks[7], (DEPTH, D_MODEL), jnp.float32)
    ln1_b = 0.02 * nrm(ks[8], (DEPTH, D_MODEL), jnp.float32)
    router_w = nrm(ks[9], (DEPTH, D_MODEL, N_EXPERTS), jnp.float32) * D_MODEL ** -0.5
    router_b = 0.01 * nrm(ks[10], (DEPTH, N_EXPERTS), jnp.float32)
    expert_w_up = nrm(ks[11], (DEPTH, N_EXPERTS, D_MODEL, 2 * D_FF), jnp.float32) * D_MODEL ** -0.5 * beta
    expert_b_up = 0.01 * nrm(ks[12], (DEPTH, N_EXPERTS, 2 * D_FF), jnp.float32)
    expert_w_down = nrm(ks[13], (DEPTH, N_EXPERTS, D_FF, D_MODEL), jnp.float32) * D_FF ** -0.5 * beta
    expert_b_down = 0.01 * nrm(ks[14], (DEPTH, N_EXPERTS, D_MODEL), jnp.float32)
    ln2_g = 1.0 + 0.05 * nrm(ks[15], (DEPTH, D_MODEL), jnp.float32)
    ln2_b = 0.02 * nrm(ks[16], (DEPTH, D_MODEL), jnp.float32)
    return {'x': x, 'w_in': w_in, 'hgrn_lb_logits': hgrn_lb_logits, 'hgrn_norm_g': hgrn_norm_g,
            'w_branch_sb': w_branch_sb, 'w_branch_hgrn': w_branch_hgrn, 'w_out': w_out,
            'ln1_g': ln1_g, 'ln1_b': ln1_b, 'router_w': router_w, 'router_b': router_b,
            'expert_w_up': expert_w_up, 'expert_b_up': expert_b_up,
            'expert_w_down': expert_w_down, 'expert_b_down': expert_b_down,
            'ln2_g': ln2_g, 'ln2_b': ln2_b}


def reference(x, w_in, hgrn_lb_logits, hgrn_norm_g, w_branch_sb, w_branch_hgrn, w_out,
              ln1_g, ln1_b, router_w, router_b, expert_w_up, expert_b_up,
              expert_w_down, expert_b_down, ln2_g, ln2_b):
    lower_bounds = jnp.cumsum(jax.nn.softmax(hgrn_lb_logits.astype(jnp.float32), axis=0), axis=0)
    h = x
    for l in range(DEPTH):
        mix = hybrid_mixer(h, w_in[l], lower_bounds[l], hgrn_norm_g[l],
                           w_branch_sb[l], w_branch_hgrn[l], w_out[l])
        h = layer_norm(DEEPNORM_ALPHA * h + mix, ln1_g[l], ln1_b[l])
        ffn = moe(h, router_w[l], router_b[l], expert_w_up[l], expert_b_up[l],
                  expert_w_down[l], expert_b_down[l])
        h = layer_norm(DEEPNORM_ALPHA * h + ffn, ln2_g[l], ln2_b[l])
    return h
```

```python
import functools
import math

import jax
import jax.numpy as jnp
import numpy as np
from jax import lax
from jax.experimental import pallas as pl
from jax.experimental.pallas import tpu as pltpu

F32 = jnp.float32
BF16 = jnp.bfloat16
I32 = jnp.int32

LANES = 128
SUBLANES = 8
VMEM_LIMIT_BYTES = 56 * 1024 * 1024

SB_HEAD_DIM = 64
HGRN_HEAD_DIM = 128
N_EXPERTS = 32
TOP_K = 4
SWIGLU_ALPHA = 1.702
SWIGLU_LIMIT = 7.0
DEPTH = 1
DEEPNORM_ALPHA = (2 * DEPTH) ** 0.25
LN_EPS = 1e-5
RMS_EPS = 1e-6

INPROJ_ROWS = 512
ATTN_BLOCK = 256
HGRN_CHUNK = 64
HGRN_STEP_ROWS = 256
HGRN_DIRECT = 8
MIX_ROWS = 256
DISPATCH_ROWS = 256
EXPERT_ROWS = 256
FF_CHUNK = 512
COMBINE_ROWS = 128
NEG_BIG = -1e30


def _cparams(semantics):
    return pltpu.CompilerParams(dimension_semantics=semantics, vmem_limit_bytes=VMEM_LIMIT_BYTES)


def _split_bf16(a):
    hi = a.astype(BF16)
    lo = (a - hi.astype(F32)).astype(BF16)
    return hi, lo


def _dot_nt(a, b):
    return lax.dot_general(a, b, (((1,), (1,)), ((), ())), preferred_element_type=F32)


def _dot_tn(a, b):
    return lax.dot_general(a, b, (((0,), (0,)), ((), ())), preferred_element_type=F32)


def _dot(a, b):
    return jnp.dot(a, b, preferred_element_type=F32)


def _inproj_kernel(x_ref, w_ref, *out_refs, widths):
    xb = x_ref[...].astype(BF16)
    off = 0
    for ref, width in zip(out_refs, widths):
        ref[...] = _dot(xb, w_ref[:, off:off + width]).astype(ref.dtype)
        off += width


def _inproj(x2, w_bf, widths):
    n, d = x2.shape
    tm = INPROJ_ROWS
    out_shape = [jax.ShapeDtypeStruct((n, w), BF16) for w in widths]
    out_specs = [pl.BlockSpec((tm, w), lambda i: (i, 0)) for w in widths]
    return pl.pallas_call(
        functools.partial(_inproj_kernel, widths=widths),
        grid=(n // tm,),
        in_specs=[pl.BlockSpec((tm, d), lambda i: (i, 0)),
                  pl.BlockSpec(w_bf.shape, lambda i: (0, 0))],
        out_specs=out_specs,
        out_shape=out_shape,
        compiler_params=_cparams(("arbitrary",)),
        name="inproj",
    )(x2, w_bf)


def _attn_kernel(q_ref, k_ref, v_ref, tri_ref, o_ref, *, blk, scale):
    qi = pl.program_id(2)
    q = q_ref[0].astype(F32) * scale
    lane = lax.broadcasted_iota(I32, q.shape, 1)
    first = lane < SB_HEAD_DIM
    q_heads = (jnp.where(first, q, 0.0).astype(BF16), jnp.where(first, 0.0, q).astype(BF16))
    tri = tri_ref[...]
    row = lax.broadcasted_iota(I32, (blk, blk), 0)
    col = lax.broadcasted_iota(I32, (blk, blk), 1)
    strictly_before = col < row

    def tile(kb, carry, diagonal):
        start = pl.multiple_of(kb * blk, blk)
        ks = k_ref[0, pl.ds(start, blk), :]
        vs = v_ref[0, pl.ds(start, blk), :]
        new = []
        for qh, (acc, csum) in zip(q_heads, carry):
            z = _dot_nt(qh, ks)
            log_keep = jnp.minimum(-z, 0.0) - jnp.log1p(jnp.exp(-jnp.abs(z)))
            if diagonal:
                log_keep = jnp.where(strictly_before, log_keep, 0.0)
            hi, lo = _split_bf16(log_keep)
            log_after = _dot(hi, tri) + _dot(lo, tri) + csum
            w = jnp.exp(z + log_keep + log_after)
            if diagonal:
                w = jnp.where(strictly_before, w, 0.0)
            acc = acc + _dot(w.astype(BF16), vs)
            csum = csum + jnp.sum(log_keep, axis=1, keepdims=True)
            new.append((acc, csum))
        return tuple(new)

    zero = (jnp.zeros((blk, LANES), F32), jnp.zeros((blk, 1), F32))
    carry = tile(qi, (zero, zero), True)
    carry = lax.fori_loop(0, qi, lambda j, c: tile(qi - 1 - j, c, False), carry)
    o_ref[0] = jnp.where(first, carry[0][0], carry[1][0]).astype(o_ref.dtype)


def _attention(q, k, v):
    b, t, w = q.shape
    blk = ATTN_BLOCK
    pairs = w // LANES
    idx = np.arange(blk)
    tri = jnp.asarray(idx[:, None] > idx[None, :], BF16)
    return pl.pallas_call(
        functools.partial(_attn_kernel, blk=blk, scale=SB_HEAD_DIM ** -0.5),
        grid=(b, pairs, t // blk),
        in_specs=[pl.BlockSpec((1, blk, LANES), lambda bi, hp, qi: (bi, qi, hp)),
                  pl.BlockSpec((1, t, LANES), lambda bi, hp, qi: (bi, 0, hp)),
                  pl.BlockSpec((1, t, LANES), lambda bi, hp, qi: (bi, 0, hp)),
                  pl.BlockSpec((blk, blk), lambda bi, hp, qi: (0, 0))],
        out_specs=pl.BlockSpec((1, blk, LANES), lambda bi, hp, qi: (bi, qi, hp)),
        out_shape=jax.ShapeDtypeStruct((b, t, w), BF16),
        compiler_params=_cparams(("arbitrary", "arbitrary", "arbitrary")),
        name="attn",
    )(q, k, v, tri)


def _hgrn_level_sizes():
    sizes = []
    s = HGRN_CHUNK
    while s > HGRN_DIRECT:
        sizes.append(s)
        s //= 2
    return tuple(sizes)


def _hgrn_kernel(hq_ref, hf_ref, hi_ref, hg_ref, lbl_ref, ng_ref, tri_ref, mask_ref, o_ref, st_ref,
                 *, chunks):
    c = HGRN_CHUNK
    d = HGRN_DIRECT
    groups = c // d

    @pl.when(pl.program_id(2) == 0)
    def _():
        st_ref[...] = jnp.zeros_like(st_ref)

    logits = lbl_ref[...].astype(F32)
    ex = jnp.exp(logits - jnp.max(logits, axis=0, keepdims=True))
    lb = ex[0:1, :] / jnp.sum(ex, axis=0, keepdims=True)
    norm_g = ng_ref[...]
    tri = tri_ref[...]
    sub = lax.broadcasted_iota(I32, (groups, d, LANES), 1)

    for ci in range(chunks):
        rows = slice(ci * c, (ci + 1) * c)
        f = lb + (1.0 - lb) * jax.nn.sigmoid(hf_ref[0, rows, :].astype(F32))
        g = jnp.log(f)
        kk = 1.0 - f
        qq = jax.nn.silu(hq_ref[0, rows, :].astype(F32))
        vv = hi_ref[0, rows, :].astype(F32)
        vb = vv.astype(BF16)

        g_hi, g_lo = _split_bf16(g)
        bcum = _dot(tri, g_hi) + _dot(tri, g_lo)
        b_last = bcum[c - 1:c, :]

        st = st_ref[...]
        o = _dot_nt((qq * jnp.exp(bcum)).astype(BF16), st.astype(BF16))

        scores = jnp.zeros((c, c), F32)
        for li, size in enumerate(_hgrn_level_sizes()):
            half = size // 2
            ref_rows = jnp.concatenate(
                [jnp.broadcast_to(bcum[bi * size + half - 1:bi * size + half, :], (size, LANES))
                 for bi in range(c // size)], axis=0)
            q_dec = qq * jnp.exp(jnp.minimum(bcum - ref_rows, 0.0))
            k_dec = kk * jnp.exp(jnp.minimum(ref_rows - bcum, 0.0))
            sc = _dot_nt(q_dec.astype(BF16), k_dec.astype(BF16))
            scores = scores + jnp.where(mask_ref[li] > 0.0, sc, 0.0)
        o = o + _dot(scores.astype(BF16), vb)

        b3 = bcum.reshape(groups, d, LANES)
        q3 = qq.reshape(groups, d, LANES)
        k3 = kk.reshape(groups, d, LANES)
        v3 = vv.reshape(groups, d, LANES)
        od = jnp.zeros((groups, d, LANES), F32)
        for j in range(d):
            e = jnp.exp(jnp.minimum(b3 - b3[:, j:j + 1, :], 0.0))
            colj = jnp.sum(q3 * e * k3[:, j:j + 1, :], axis=-1, keepdims=True)
            colj = jnp.where(sub[:, :, 0:1] >= j, colj, 0.0)
            od = od + colj * v3[:, j:j + 1, :]
        o = o + od.reshape(c, LANES)

        k_dec = kk * jnp.exp(b_last - bcum)
        st_ref[...] = st * jnp.exp(b_last) + _dot_tn(vb, k_dec.astype(BF16))

        o = o * lax.rsqrt(jnp.mean(jnp.square(o), axis=-1, keepdims=True) + RMS_EPS)
        o = o * norm_g * jax.nn.sigmoid(hg_ref[0, rows, :].astype(F32))
        o_ref[0, rows, :] = o.astype(o_ref.dtype)


def _hgrn(hq, hf, hi, hg, lb_logits, norm_g):
    b, t, w = hq.shape
    heads = w // HGRN_HEAD_DIM
    c = HGRN_CHUNK
    rows = min(HGRN_STEP_ROWS, t)
    idx = np.arange(c)
    tri = jnp.asarray(idx[:, None] >= idx[None, :], BF16)
    masks = []
    for size in _hgrn_level_sizes():
        half = size // 2
        same = (idx[:, None] // size) == (idx[None, :] // size)
        masks.append(same & ((idx[:, None] % size) >= half) & ((idx[None, :] % size) < half))
    masks = jnp.asarray(np.stack(masks), F32)
    seq_spec = pl.BlockSpec((1, rows, LANES), lambda bi, h, ti: (bi, ti, h))
    return pl.pallas_call(
        functools.partial(_hgrn_kernel, chunks=rows // c),
        grid=(b, heads, t // rows),
        in_specs=[seq_spec, seq_spec, seq_spec, seq_spec,
                  pl.BlockSpec((lb_logits.shape[0], LANES), lambda bi, h, ti: (0, h)),
                  pl.BlockSpec((1, LANES), lambda bi, h, ti: (0, h)),
                  pl.BlockSpec((c, c), lambda bi, h, ti: (0, 0)),
                  pl.BlockSpec(masks.shape, lambda bi, h, ti: (0, 0, 0))],
        out_specs=seq_spec,
        out_shape=jax.ShapeDtypeStruct((b, t, w), BF16),
        scratch_shapes=[pltpu.VMEM((HGRN_HEAD_DIM, HGRN_HEAD_DIM), F32)],
        compiler_params=_cparams(("arbitrary", "arbitrary", "arbitrary")),
        name="hgrn",
    )(hq, hf, hi, hg, lb_logits, norm_g, tri, masks)


def _layer_norm(v, g, b):
    mu = jnp.mean(v, axis=-1, keepdims=True)
    cen = v - mu
    var = jnp.mean(jnp.square(cen), axis=-1, keepdims=True)
    return cen * lax.rsqrt(var + LN_EPS) * g + b


def _mix_kernel(x_ref, osb_ref, ohg_ref, gsb_ref, ghg_ref, wsb_ref, whg_ref, wout_ref,
                g1_ref, b1_ref, rw_ref, rb_ref, tri_ref,
                h_ref, idx_ref, gate_ref, pos_ref, cnt_ref, run_ref):
    @pl.when(pl.program_id(0) == 0)
    def _():
        run_ref[...] = jnp.zeros_like(run_ref)

    merged = (jax.nn.sigmoid(gsb_ref[...].astype(F32)) * _dot(osb_ref[...], wsb_ref[...])
              + jax.nn.sigmoid(ghg_ref[...].astype(F32)) * _dot(ohg_ref[...], whg_ref[...]))
    mix = _dot(merged.astype(BF16), wout_ref[...])
    h = _layer_norm(DEEPNORM_ALPHA * x_ref[...] + mix, g1_ref[...], b1_ref[...])
    h_ref[...] = h

    h_hi, h_lo = _split_bf16(h)
    h_lo2 = (h - h_hi.astype(F32) - h_lo.astype(F32)).astype(BF16)
    rw = rw_ref[...]
    rw_hi, rw_lo = _split_bf16(rw)
    rw_lo2 = (rw - rw_hi.astype(F32) - rw_lo.astype(F32)).astype(BF16)
    logits = (_dot(h_hi, rw_hi) + (_dot(h_hi, rw_lo) + _dot(h_lo, rw_hi))
              + (_dot(h_lo, rw_lo) + _dot(h_hi, rw_lo2) + _dot(h_lo2, rw_hi))) + rb_ref[...]

    tm = logits.shape[0]
    lane = lax.broadcasted_iota(I32, (tm, LANES), 1)
    remaining = logits
    vals, idxs, hots = [], [], []
    for _ in range(TOP_K):
        m = jnp.max(remaining, axis=-1, keepdims=True)
        sel = jnp.min(jnp.where(remaining == m, lane, LANES), axis=-1, keepdims=True)
        hot = lane == sel
        remaining = jnp.where(hot, NEG_BIG, remaining)
        vals.append(m)
        idxs.append(sel)
        hots.append(hot)
    exps = [jnp.exp(v - vals[0]) for v in vals]
    denom = exps[0] + exps[1] + exps[2] + exps[3]

    chosen = jnp.where(hots[0] | hots[1] | hots[2] | hots[3], 1.0, 0.0)
    before = _dot(tri_ref[...], chosen.astype(BF16)) + run_ref[0:1, :]
    idx_slab = jnp.zeros((tm, LANES), I32)
    gate_slab = jnp.zeros((tm, LANES), F32)
    pos_slab = jnp.zeros((tm, LANES), I32)
    for kk in range(TOP_K):
        rank = jnp.sum(jnp.where(hots[kk], before, 0.0), axis=-1, keepdims=True)
        here = lane == kk
        idx_slab = jnp.where(here, idxs[kk], idx_slab)
        gate_slab = jnp.where(here, exps[kk] / denom, gate_slab)
        pos_slab = jnp.where(here, rank.astype(I32), pos_slab)
    idx_ref[...] = idx_slab
    gate_ref[...] = gate_slab
    pos_ref[...] = pos_slab
    run_ref[0:1, :] = run_ref[0:1, :] + jnp.sum(chosen, axis=0, keepdims=True)
    cnt_ref[...] = run_ref[...]


def _mix(x2, o_sb, o_hg, g_sb, g_hg, wsb, whg, wout, ln_g, ln_b, rw_pad, rb_pad):
    n, d = x2.shape
    tm = MIX_ROWS
    idx = np.arange(tm)
    tri = jnp.asarray(idx[:, None] > idx[None, :], BF16)
    row = lambda w: pl.BlockSpec((tm, w), lambda i: (i, 0))
    full = lambda a: pl.BlockSpec(a.shape, lambda i: (0,) * a.ndim)
    return pl.pallas_call(
        _mix_kernel,
        grid=(n // tm,),
        in_specs=[row(d), row(o_sb.shape[1]), row(o_hg.shape[1]), row(d), row(d),
                  full(wsb), full(whg), full(wout), full(ln_g), full(ln_b),
                  full(rw_pad), full(rb_pad), full(tri)],
        out_specs=[row(d), row(LANES), row(LANES), row(LANES),
                   pl.BlockSpec((SUBLANES, LANES), lambda i: (0, 0))],
        out_shape=[jax.ShapeDtypeStruct((n, d), F32),
                   jax.ShapeDtypeStruct((n, LANES), I32),
                   jax.ShapeDtypeStruct((n, LANES), F32),
                   jax.ShapeDtypeStruct((n, LANES), I32),
                   jax.ShapeDtypeStruct((SUBLANES, LANES), F32)],
        scratch_shapes=[pltpu.VMEM((SUBLANES, LANES), F32)],
        compiler_params=_cparams(("arbitrary",)),
        name="mix",
    )(x2, o_sb, o_hg, g_sb, g_hg, wsb, whg, wout, ln_g, ln_b, rw_pad, rb_pad, tri)


def _row_copy(src_ref, src_row, dst_ref, dst_row, sem):
    return pltpu.make_async_copy(src_ref.at[pl.ds(src_row, 1)], dst_ref.at[pl.ds(dst_row, 1)], sem)


def _dispatch_kernel(dest_ref, h_ref, xs_in_ref, xs_ref, sem, *, rows):
    del xs_in_ref
    base = pl.program_id(0) * rows

    def issue(r, _):
        for kk in range(TOP_K):
            _row_copy(h_ref, r, xs_ref, dest_ref[(base + r) * TOP_K + kk], sem).start()
        return 0

    lax.fori_loop(0, rows, issue, 0)
    for _ in range(TOP_K):
        pltpu.make_async_copy(h_ref, xs_ref.at[pl.ds(0, rows)], sem).wait()


def _dispatch(dest_flat, h, n_slots):
    n, d = h.shape
    rows = DISPATCH_ROWS
    grid_spec = pltpu.PrefetchScalarGridSpec(
        num_scalar_prefetch=1,
        grid=(n // rows,),
        in_specs=[pl.BlockSpec((rows, d), lambda i, dest: (i, 0)),
                  pl.BlockSpec(memory_space=pl.ANY)],
        out_specs=pl.BlockSpec(memory_space=pl.ANY),
        scratch_shapes=[pltpu.SemaphoreType.DMA],
    )
    return pl.pallas_call(
        functools.partial(_dispatch_kernel, rows=rows),
        grid_spec=grid_spec,
        out_shape=jax.ShapeDtypeStruct((n_slots, d), F32),
        input_output_aliases={2: 0},
        compiler_params=_cparams(("arbitrary",)),
        name="dispatch",
    )(dest_flat, h, jnp.zeros((n_slots, d), F32))


def _expert_kernel(be_ref, nused_ref, xs_ref, wg_ref, wl_ref, bg_ref, bl_ref, wd_ref, bd_ref, y_ref):
    i = pl.program_id(0)

    @pl.when(i < nused_ref[0])
    def _():
        xb = xs_ref[...].astype(BF16)
        d_ff = wg_ref.shape[2]
        acc = jnp.zeros(y_ref.shape, F32) + bd_ref[0]
        for c0 in range(0, d_ff, FF_CHUNK):
            cols = slice(c0, c0 + FF_CHUNK)
            x_glu = jnp.minimum(_dot(xb, wg_ref[0, :, cols]) + bg_ref[0, :, cols], SWIGLU_LIMIT)
            x_lin = jnp.clip(_dot(xb, wl_ref[0, :, cols]) + bl_ref[0, :, cols],
                             -SWIGLU_LIMIT, SWIGLU_LIMIT)
            act = x_glu * jax.nn.sigmoid(SWIGLU_ALPHA * x_glu) * (x_lin + 1.0)
            acc = acc + _dot(act.astype(BF16), wd_ref[0, cols, :])
        y_ref[...] = acc

    @pl.when(i >= nused_ref[0])
    def _():
        y_ref[...] = jnp.zeros_like(y_ref)


def _experts(block_e, n_used, xs, wg, wl, bg, bl, wd, bd):
    n_slots, d = xs.shape
    tm = EXPERT_ROWS
    d_ff = wg.shape[2]
    by_e3 = lambda i, be, nu: (be[i], 0, 0)
    grid_spec = pltpu.PrefetchScalarGridSpec(
        num_scalar_prefetch=2,
        grid=(n_slots // tm,),
        in_specs=[pl.BlockSpec((tm, d), lambda i, be, nu: (i, 0)),
                  pl.BlockSpec((1, d, d_ff), by_e3),
                  pl.BlockSpec((1, d, d_ff), by_e3),
                  pl.BlockSpec((1, 1, d_ff), by_e3),
                  pl.BlockSpec((1, 1, d_ff), by_e3),
                  pl.BlockSpec((1, d_ff, d), by_e3),
                  pl.BlockSpec((1, 1, d), by_e3)],
        out_specs=pl.BlockSpec((tm, d), lambda i, be, nu: (i, 0)),
    )
    return pl.pallas_call(
        _expert_kernel,
        grid_spec=grid_spec,
        out_shape=jax.ShapeDtypeStruct((n_slots, d), F32),
        compiler_params=_cparams(("arbitrary",)),
        name="experts",
    )(block_e, n_used, xs, wg, wl, bg, bl, wd, bd)


def _combine_kernel(dest_ref, y_ref, h_ref, gate_ref, g2_ref, b2_ref, o_ref, buf_ref, sem, *, rows):
    base = pl.program_id(0) * rows

    def issue(r, _):
        for kk in range(TOP_K):
            _row_copy(y_ref, dest_ref[(base + r) * TOP_K + kk], buf_ref.at[kk], r, sem).start()
        return 0

    lax.fori_loop(0, rows, issue, 0)
    for kk in range(TOP_K):
        pltpu.make_async_copy(y_ref.at[pl.ds(0, rows)], buf_ref.at[kk], sem).wait()

    gates = gate_ref[...]
    ffn = jnp.zeros(o_ref.shape, F32)
    for kk in range(TOP_K):
        ffn = ffn + buf_ref[kk] * gates[:, kk:kk + 1]
    o_ref[...] = _layer_norm(DEEPNORM_ALPHA * h_ref[...] + ffn, g2_ref[...], b2_ref[...])


def _combine(dest_flat, y, h, gates, ln_g, ln_b):
    n, d = h.shape
    rows = COMBINE_ROWS
    grid_spec = pltpu.PrefetchScalarGridSpec(
        num_scalar_prefetch=1,
        grid=(n // rows,),
        in_specs=[pl.BlockSpec(memory_space=pl.ANY),
                  pl.BlockSpec((rows, d), lambda i, dest: (i, 0)),
                  pl.BlockSpec((rows, LANES), lambda i, dest: (i, 0)),
                  pl.BlockSpec(ln_g.shape, lambda i, dest: (0, 0)),
                  pl.BlockSpec(ln_b.shape, lambda i, dest: (0, 0))],
        out_specs=pl.BlockSpec((rows, d), lambda i, dest: (i, 0)),
        scratch_shapes=[pltpu.VMEM((TOP_K, rows, d), F32), pltpu.SemaphoreType.DMA],
    )
    return pl.pallas_call(
        functools.partial(_combine_kernel, rows=rows),
        grid_spec=grid_spec,
        out_shape=jax.ShapeDtypeStruct((n, d), F32),
        compiler_params=_cparams(("arbitrary",)),
        name="combine",
    )(dest_flat, y, h, gates, ln_g, ln_b)


def kernel(x, w_in, hgrn_lb_logits, hgrn_norm_g, w_branch_sb, w_branch_hgrn, w_out, ln1_g, ln1_b,
           router_w, router_b, expert_w_up, expert_b_up, expert_w_down, expert_b_down, ln2_g, ln2_b):
    b, t, d = x.shape
    assert w_in.shape[0] == DEPTH and hgrn_lb_logits.shape[0] == DEPTH + 1
    n = b * t
    sbw = w_branch_sb.shape[1]
    hgw = w_branch_hgrn.shape[1]
    widths = (sbw, sbw, sbw, hgw, hgw, hgw, hgw, d, d)
    assert sum(widths) == w_in.shape[2]
    n_exp, _, two_ff = expert_w_up.shape[1:]
    d_ff = two_ff // 2
    assert n_exp == N_EXPERTS and n % EXPERT_ROWS == 0

    x2 = x.reshape(n, d)
    sq, sk, sv, hq, hf, hi, hg, g_sb, g_hg = _inproj(x2, w_in[0].astype(BF16), widths)

    seq = lambda a: a.reshape(b, t, a.shape[-1])
    o_sb = _attention(seq(sq), seq(sk), seq(sv)).reshape(n, sbw)
    o_hg = _hgrn(seq(hq), seq(hf), seq(hi), seq(hg), hgrn_lb_logits, hgrn_norm_g).reshape(n, hgw)

    rw_pad = jnp.zeros((d, LANES), F32).at[:, :n_exp].set(router_w[0])
    rb_pad = jnp.full((1, LANES), NEG_BIG, F32).at[0, :n_exp].set(router_b[0])
    h, idx_slab, gate_slab, pos_slab, cnt = _mix(
        x2, o_sb, o_hg, g_sb, g_hg,
        w_branch_sb[0].astype(BF16), w_branch_hgrn[0].astype(BF16), w_out[0].astype(BF16),
        ln1_g, ln1_b, rw_pad, rb_pad)

    counts = cnt[0, :n_exp].astype(I32)
    padded = (counts + EXPERT_ROWS - 1) // EXPERT_ROWS * EXPERT_ROWS
    pad_end = jnp.cumsum(padded)
    pad_start = pad_end - padded
    top_idx = idx_slab[:, :TOP_K]
    dest = pad_start[top_idx] + pos_slab[:, :TOP_K]
    dest_flat = dest.reshape(-1).astype(I32)
    n_blocks = -(-(n * TOP_K + n_exp * (EXPERT_ROWS - 1)) // EXPERT_ROWS)
    n_slots = n_blocks * EXPERT_ROWS
    block_row0 = jnp.arange(n_blocks, dtype=I32) * EXPERT_ROWS
    block_e = jnp.minimum(
        jnp.sum((pad_end[None, :] <= block_row0[:, None]).astype(I32), axis=1), n_exp - 1)
    n_used = (pad_end[-1:] // EXPERT_ROWS).astype(I32)

    xs = _dispatch(dest_flat, h, n_slots)

    wg = expert_w_up[0][:, :, 0::2].astype(BF16)
    wl = expert_w_up[0][:, :, 1::2].astype(BF16)
    bg = expert_b_up[0][:, None, 0::2]
    bl = expert_b_up[0][:, None, 1::2]
    wd = expert_w_down[0].astype(BF16)
    bd = expert_b_down[0][:, None, :]
    y = _experts(block_e, n_used, xs, wg, wl, bg, bl, wd, bd)

    out = _combine(dest_flat, y, h, gate_slab, ln2_g, ln2_b)
    return out.reshape(b, t, d)
```

```python
import functools
import math

import jax
import jax.numpy as jnp
import numpy as np
from jax import lax
from jax.experimental import pallas as pl
from jax.experimental.pallas import tpu as pltpu

F32 = jnp.float32
BF16 = jnp.bfloat16
I32 = jnp.int32

LANES = 128
SUBLANES = 8
PAIR_GROUP = 2 * LANES
VMEM_LIMIT_BYTES = 56 * 1024 * 1024

SB_HEAD_DIM = 64
HGRN_HEAD_DIM = 128
N_EXPERTS = 32
TOP_K = 4
SWIGLU_ALPHA = 1.702
SWIGLU_LIMIT = 7.0
DEPTH = 1
DEEPNORM_ALPHA = (2 * DEPTH) ** 0.25
LN_EPS = 1e-5
RMS_EPS = 1e-6

INPROJ_ROWS = 512
ATTN_BLOCK = 256
HGRN_CHUNK = 64
HGRN_STEP_ROWS = 256
HGRN_DIRECT = 8
MIX_ROWS = 256
DISPATCH_ROWS = 256
EXPERT_ROWS = 256
FF_CHUNK = 512
COMBINE_ROWS = 128
NEG_BIG = -1e30
LOG2E = math.log2(math.e)


def _cparams(semantics):
    return pltpu.CompilerParams(dimension_semantics=semantics, vmem_limit_bytes=VMEM_LIMIT_BYTES)


def _split_bf16(a):
    hi = a.astype(BF16)
    lo = (a - hi.astype(F32)).astype(BF16)
    return hi, lo


def _split_trunc(a):
    bits = lax.bitcast_convert_type(a, jnp.uint32) & jnp.uint32(0xFFFF0000)
    hi = lax.bitcast_convert_type(bits, F32)
    return hi.astype(BF16), (a - hi).astype(BF16)


def _dot_nt(a, b):
    return lax.dot_general(a, b, (((1,), (1,)), ((), ())), preferred_element_type=F32)


def _dot_tn(a, b):
    return lax.dot_general(a, b, (((0,), (0,)), ((), ())), preferred_element_type=F32)


def _dot(a, b):
    return jnp.dot(a, b, preferred_element_type=F32)


def _inproj_kernel(x_ref, w_ref, *out_refs, widths):
    xb = x_ref[...].astype(BF16)
    off = 0
    for ref, width in zip(out_refs, widths):
        ref[...] = _dot(xb, w_ref[:, off:off + width]).astype(ref.dtype)
        off += width


def _inproj(x2, w_bf, widths):
    n, d = x2.shape
    tm = INPROJ_ROWS
    out_shape = [jax.ShapeDtypeStruct((n, w), BF16) for w in widths]
    out_specs = [pl.BlockSpec((tm, w), lambda i: (i, 0)) for w in widths]
    return pl.pallas_call(
        functools.partial(_inproj_kernel, widths=widths),
        grid=(n // tm,),
        in_specs=[pl.BlockSpec((tm, d), lambda i: (i, 0)),
                  pl.BlockSpec(w_bf.shape, lambda i: (0, 0))],
        out_specs=out_specs,
        out_shape=out_shape,
        compiler_params=_cparams(("arbitrary",)),
        name="inproj",
    )(x2, w_bf)


def _attn_kernel(q_ref, k_ref, v_ref, tri_ref, o_ref, *, blk, scale):
    qi = pl.program_id(2)
    q = q_ref[0].astype(F32) * scale
    lane = lax.broadcasted_iota(I32, q.shape, 1)
    first = lane < SB_HEAD_DIM
    q_heads = (jnp.where(first, q, 0.0).astype(BF16), jnp.where(first, 0.0, q).astype(BF16))
    tri2 = tri_ref[...]
    row = lax.broadcasted_iota(I32, (blk, blk), 0)
    col = lax.broadcasted_iota(I32, (blk, blk), 1)
    strictly_before = col < row
    heads = range(len(q_heads))

    def scores(kb):
        start = pl.multiple_of(kb * blk, blk)
        ks = k_ref[0, pl.ds(start, blk), :]
        return tuple(_dot_nt(qh, ks) for qh in q_heads)

    def finish(kb, zs, carry, diagonal):
        start = pl.multiple_of(kb * blk, blk)
        vs = v_ref[0, pl.ds(start, blk), :]
        ts = [jnp.log(1.0 + jnp.exp2(jnp.abs(z) * -LOG2E)) for z in zs]
        log_beta = [jnp.minimum(zs[h], 0.0) - ts[h] for h in heads]
        log_keep = [log_beta[h] - zs[h] for h in heads]
        if diagonal:
            log_keep = [jnp.where(strictly_before, lk, 0.0) for lk in log_keep]
        parts = [jnp.concatenate(_split_trunc(lk), axis=1) for lk in log_keep]
        log_after = [_dot(parts[h], tri2) + carry[h][1] for h in heads]
        ws = [jnp.exp(log_beta[h] + log_after[h]) for h in heads]
        if diagonal:
            ws = [jnp.where(strictly_before, w, 0.0) for w in ws]
        return tuple((carry[h][0] + _dot(ws[h].astype(BF16), vs),
                      carry[h][1] + jnp.sum(log_keep[h], axis=1, keepdims=True)) for h in heads)

    zero = (jnp.zeros((blk, LANES), F32), jnp.zeros((blk, 1), F32))
    carry = finish(qi, scores(qi), (zero, zero), True)

    def body(j, state):
        zs, carry = state
        kb = qi - 1 - j
        zs_next = scores(jnp.maximum(kb - 1, 0))
        return zs_next, finish(kb, zs, carry, False)

    _, carry = lax.fori_loop(0, qi, body, (scores(jnp.maximum(qi - 1, 0)), carry))
    o_ref[0] = jnp.where(first, carry[0][0], carry[1][0]).astype(o_ref.dtype)


def _attention(q, k, v):
    b, t, w = q.shape
    blk = ATTN_BLOCK
    pairs = w // LANES
    idx = np.arange(blk)
    tri = idx[:, None] > idx[None, :]
    tri = jnp.asarray(np.concatenate([tri, tri], axis=0), BF16)
    return pl.pallas_call(
        functools.partial(_attn_kernel, blk=blk, scale=SB_HEAD_DIM ** -0.5),
        grid=(b, pairs, t // blk),
        in_specs=[pl.BlockSpec((1, blk, LANES), lambda bi, hp, qi: (bi, qi, hp)),
                  pl.BlockSpec((1, t, LANES), lambda bi, hp, qi: (bi, 0, hp)),
                  pl.BlockSpec((1, t, LANES), lambda bi, hp, qi: (bi, 0, hp)),
                  pl.BlockSpec((2 * blk, blk), lambda bi, hp, qi: (0, 0))],
        out_specs=pl.BlockSpec((1, blk, LANES), lambda bi, hp, qi: (bi, qi, hp)),
        out_shape=jax.ShapeDtypeStruct((b, t, w), BF16),
        compiler_params=_cparams(("arbitrary", "arbitrary", "arbitrary")),
        name="attn",
    )(q, k, v, tri)


def _hgrn_level_sizes():
    sizes = []
    s = HGRN_CHUNK
    while s > HGRN_DIRECT:
        sizes.append(s)
        s //= 2
    return tuple(sizes)


def _hgrn_kernel(hq_ref, hf_ref, hi_ref, hg_ref, lbl_ref, ng_ref, tri_ref, mask_ref, o_ref, st_ref,
                 *, chunks):
    c = HGRN_CHUNK
    d = HGRN_DIRECT
    groups = c // d

    @pl.when(pl.program_id(2) == 0)
    def _():
        st_ref[...] = jnp.zeros_like(st_ref)

    logits = lbl_ref[...].astype(F32)
    ex = jnp.exp(logits - jnp.max(logits, axis=0, keepdims=True))
    lb = ex[0:1, :] / jnp.sum(ex, axis=0, keepdims=True)
    norm_g = ng_ref[...]
    tri = tri_ref[...]
    sub = lax.broadcasted_iota(I32, (groups, d, LANES), 1)

    for ci in range(chunks):
        rows = slice(ci * c, (ci + 1) * c)
        f = lb + (1.0 - lb) * jax.nn.sigmoid(hf_ref[0, rows, :].astype(F32))
        g = jnp.log(f)
        kk = 1.0 - f
        qq = jax.nn.silu(hq_ref[0, rows, :].astype(F32))
        vv = hi_ref[0, rows, :].astype(F32)
        vb = vv.astype(BF16)

        g_hi, g_lo = _split_bf16(g)
        bcum = _dot(tri, g_hi) + _dot(tri, g_lo)
        b_last = bcum[c - 1:c, :]

        st = st_ref[...]
        o = _dot_nt((qq * jnp.exp(bcum)).astype(BF16), st.astype(BF16))

        scores = jnp.zeros((c, c), F32)
        for li, size in enumerate(_hgrn_level_sizes()):
            half = size // 2
            ref_rows = jnp.concatenate(
                [jnp.broadcast_to(bcum[bi * size + half - 1:bi * size + half, :], (size, LANES))
                 for bi in range(c // size)], axis=0)
            q_dec = qq * jnp.exp(jnp.minimum(bcum - ref_rows, 0.0))
            k_dec = kk * jnp.exp(jnp.minimum(ref_rows - bcum, 0.0))
            sc = _dot_nt(q_dec.astype(BF16), k_dec.astype(BF16))
            scores = scores + jnp.where(mask_ref[li] > 0.0, sc, 0.0)
        o = o + _dot(scores.astype(BF16), vb)

        b3 = bcum.reshape(groups, d, LANES)
        q3 = qq.reshape(groups, d, LANES)
        k3 = kk.reshape(groups, d, LANES)
        v3 = vv.reshape(groups, d, LANES)
        od = jnp.zeros((groups, d, LANES), F32)
        for j in range(d):
            e = jnp.exp(jnp.minimum(b3 - b3[:, j:j + 1, :], 0.0))
            colj = jnp.sum(q3 * e * k3[:, j:j + 1, :], axis=-1, keepdims=True)
            colj = jnp.where(sub[:, :, 0:1] >= j, colj, 0.0)
            od = od + colj * v3[:, j:j + 1, :]
        o = o + od.reshape(c, LANES)

        k_dec = kk * jnp.exp(b_last - bcum)
        st_ref[...] = st * jnp.exp(b_last) + _dot_tn(vb, k_dec.astype(BF16))

        o = o * lax.rsqrt(jnp.mean(jnp.square(o), axis=-1, keepdims=True) + RMS_EPS)
        o = o * norm_g * jax.nn.sigmoid(hg_ref[0, rows, :].astype(F32))
        o_ref[0, rows, :] = o.astype(o_ref.dtype)


def _hgrn(hq, hf, hi, hg, lb_logits, norm_g):
    b, t, w = hq.shape
    heads = w // HGRN_HEAD_DIM
    c = HGRN_CHUNK
    rows = min(HGRN_STEP_ROWS, t)
    idx = np.arange(c)
    tri = jnp.asarray(idx[:, None] >= idx[None, :], BF16)
    masks = []
    for size in _hgrn_level_sizes():
        half = size // 2
        same = (idx[:, None] // size) == (idx[None, :] // size)
        masks.append(same & ((idx[:, None] % size) >= half) & ((idx[None, :] % size) < half))
    masks = jnp.asarray(np.stack(masks), F32)
    seq_spec = pl.BlockSpec((1, rows, LANES), lambda bi, h, ti: (bi, ti, h))
    return pl.pallas_call(
        functools.partial(_hgrn_kernel, chunks=rows // c),
        grid=(b, heads, t // rows),
        in_specs=[seq_spec, seq_spec, seq_spec, seq_spec,
                  pl.BlockSpec((lb_logits.shape[0], LANES), lambda bi, h, ti: (0, h)),
                  pl.BlockSpec((1, LANES), lambda bi, h, ti: (0, h)),
                  pl.BlockSpec((c, c), lambda bi, h, ti: (0, 0)),
                  pl.BlockSpec(masks.shape, lambda bi, h, ti: (0, 0, 0))],
        out_specs=seq_spec,
        out_shape=jax.ShapeDtypeStruct((b, t, w), BF16),
        scratch_shapes=[pltpu.VMEM((HGRN_HEAD_DIM, HGRN_HEAD_DIM), F32)],
        compiler_params=_cparams(("arbitrary", "arbitrary", "arbitrary")),
        name="hgrn",
    )(hq, hf, hi, hg, lb_logits, norm_g, tri, masks)


def _layer_norm(v, g, b):
    mu = jnp.mean(v, axis=-1, keepdims=True)
    cen = v - mu
    var = jnp.mean(jnp.square(cen), axis=-1, keepdims=True)
    return cen * lax.rsqrt(var + LN_EPS) * g + b


def _mix_kernel(x_ref, osb_ref, ohg_ref, gsb_ref, ghg_ref, wsb_ref, whg_ref, wout_ref,
                g1_ref, b1_ref, rw_ref, rb_ref, tri_ref,
                h_ref, idx_ref, gate_ref, pos_ref, cnt_ref, run_ref):
    @pl.when(pl.program_id(0) == 0)
    def _():
        run_ref[...] = jnp.zeros_like(run_ref)

    merged = (jax.nn.sigmoid(gsb_ref[...].astype(F32)) * _dot(osb_ref[...], wsb_ref[...])
              + jax.nn.sigmoid(ghg_ref[...].astype(F32)) * _dot(ohg_ref[...], whg_ref[...]))
    mix = _dot(merged.astype(BF16), wout_ref[...])
    h = _layer_norm(DEEPNORM_ALPHA * x_ref[...] + mix, g1_ref[...], b1_ref[...])
    h_ref[...] = h

    h_hi, h_lo = _split_bf16(h)
    h_lo2 = (h - h_hi.astype(F32) - h_lo.astype(F32)).astype(BF16)
    rw = rw_ref[...]
    rw_hi, rw_lo = _split_bf16(rw)
    rw_lo2 = (rw - rw_hi.astype(F32) - rw_lo.astype(F32)).astype(BF16)
    logits = (_dot(h_hi, rw_hi) + (_dot(h_hi, rw_lo) + _dot(h_lo, rw_hi))
              + (_dot(h_lo, rw_lo) + _dot(h_hi, rw_lo2) + _dot(h_lo2, rw_hi))) + rb_ref[...]

    tm = logits.shape[0]
    lane = lax.broadcasted_iota(I32, (tm, LANES), 1)
    remaining = logits
    vals, idxs, hots = [], [], []
    for _ in range(TOP_K):
        m = jnp.max(remaining, axis=-1, keepdims=True)
        sel = jnp.min(jnp.where(remaining == m, lane, LANES), axis=-1, keepdims=True)
        hot = lane == sel
        remaining = jnp.where(hot, NEG_BIG, remaining)
        vals.append(m)
        idxs.append(sel)
        hots.append(hot)
    exps = [jnp.exp(v - vals[0]) for v in vals]
    denom = exps[0] + exps[1] + exps[2] + exps[3]

    chosen = jnp.where(hots[0] | hots[1] | hots[2] | hots[3], 1.0, 0.0)
    before = _dot(tri_ref[...], chosen.astype(BF16)) + run_ref[0:1, :]
    idx_slab = jnp.zeros((tm, LANES), I32)
    gate_slab = jnp.zeros((tm, LANES), F32)
    pos_slab = jnp.zeros((tm, LANES), I32)
    for kk in range(TOP_K):
        rank = jnp.sum(jnp.where(hots[kk], before, 0.0), axis=-1, keepdims=True)
        here = lane == kk
        idx_slab = jnp.where(here, idxs[kk], idx_slab)
        gate_slab = jnp.where(here, exps[kk] / denom, gate_slab)
        pos_slab = jnp.where(here, rank.astype(I32), pos_slab)
    idx_ref[...] = idx_slab
    gate_ref[...] = gate_slab
    pos_ref[...] = pos_slab
    run_ref[0:1, :] = run_ref[0:1, :] + jnp.sum(chosen, axis=0, keepdims=True)
    cnt_ref[...] = run_ref[...]


def _mix(x2, o_sb, o_hg, g_sb, g_hg, wsb, whg, wout, ln_g, ln_b, rw_pad, rb_pad):
    n, d = x2.shape
    tm = MIX_ROWS
    idx = np.arange(tm)
    tri = jnp.asarray(idx[:, None] > idx[None, :], BF16)
    row = lambda w: pl.BlockSpec((tm, w), lambda i: (i, 0))
    full = lambda a: pl.BlockSpec(a.shape, lambda i: (0,) * a.ndim)
    return pl.pallas_call(
        _mix_kernel,
        grid=(n // tm,),
        in_specs=[row(d), row(o_sb.shape[1]), row(o_hg.shape[1]), row(d), row(d),
                  full(wsb), full(whg), full(wout), full(ln_g), full(ln_b),
                  full(rw_pad), full(rb_pad), full(tri)],
        out_specs=[row(d), row(LANES), row(LANES), row(LANES),
                   pl.BlockSpec((SUBLANES, LANES), lambda i: (0, 0))],
        out_shape=[jax.ShapeDtypeStruct((n, d), F32),
                   jax.ShapeDtypeStruct((n, LANES), I32),
                   jax.ShapeDtypeStruct((n, LANES), F32),
                   jax.ShapeDtypeStruct((n, LANES), I32),
                   jax.ShapeDtypeStruct((SUBLANES, LANES), F32)],
        scratch_shapes=[pltpu.VMEM((SUBLANES, LANES), F32)],
        compiler_params=_cparams(("arbitrary",)),
        name="mix",
    )(x2, o_sb, o_hg, g_sb, g_hg, wsb, whg, wout, ln_g, ln_b, rw_pad, rb_pad, tri)


def _row_copy(src_ref, src_row, dst_ref, dst_row, sem):
    return pltpu.make_async_copy(src_ref.at[pl.ds(src_row, 1)], dst_ref.at[pl.ds(dst_row, 1)], sem)


def _dispatch_kernel(dest_ref, h_ref, xs_in_ref, xs_ref, sem, *, rows):
    del xs_in_ref
    base = pl.program_id(0) * rows

    def issue(r, _):
        for kk in range(TOP_K):
            _row_copy(h_ref, r, xs_ref, dest_ref[(base + r) * TOP_K + kk], sem).start()
        return 0

    lax.fori_loop(0, rows, issue, 0)
    for _ in range(TOP_K):
        pltpu.make_async_copy(h_ref, xs_ref.at[pl.ds(0, rows)], sem).wait()


def _dispatch(dest_flat, h, n_slots):
    n, d = h.shape
    rows = DISPATCH_ROWS
    grid_spec = pltpu.PrefetchScalarGridSpec(
        num_scalar_prefetch=1,
        grid=(n // rows,),
        in_specs=[pl.BlockSpec((rows, d), lambda i, dest: (i, 0)),
                  pl.BlockSpec(memory_space=pl.ANY)],
        out_specs=pl.BlockSpec(memory_space=pl.ANY),
        scratch_shapes=[pltpu.SemaphoreType.DMA],
    )
    return pl.pallas_call(
        functools.partial(_dispatch_kernel, rows=rows),
        grid_spec=grid_spec,
        out_shape=jax.ShapeDtypeStruct((n_slots, d), F32),
        input_output_aliases={2: 0},
        compiler_params=_cparams(("arbitrary",)),
        name="dispatch",
    )(dest_flat, h, jnp.zeros((n_slots, d), F32))


def _pair_split_perm():
    p = np.zeros((PAIR_GROUP, PAIR_GROUP), np.float32)
    i = np.arange(LANES)
    p[2 * i, i] = 1.0
    p[2 * i + 1, LANES + i] = 1.0
    return p


def _regroup_kernel(w_ref, perm_ref, o_ref):
    perm = perm_ref[...]
    for c0 in range(0, w_ref.shape[2], PAIR_GROUP):
        cols = slice(c0, c0 + PAIR_GROUP)
        o_ref[0, :, cols] = _dot(w_ref[0, :, cols].astype(BF16), perm).astype(o_ref.dtype)


def _regroup_up_weights(w_up):
    n_exp, d, two_ff = w_up.shape
    cols = 4 * PAIR_GROUP
    perm = jnp.asarray(_pair_split_perm(), BF16)
    return pl.pallas_call(
        _regroup_kernel,
        grid=(n_exp, two_ff // cols),
        in_specs=[pl.BlockSpec((1, d, cols), lambda e, c: (e, 0, c)),
                  pl.BlockSpec(perm.shape, lambda e, c: (0, 0))],
        out_specs=pl.BlockSpec((1, d, cols), lambda e, c: (e, 0, c)),
        out_shape=jax.ShapeDtypeStruct(w_up.shape, BF16),
        compiler_params=_cparams(("arbitrary", "arbitrary")),
        name="regroup",
    )(w_up, perm)


def _expert_kernel(be_ref, nused_ref, xs_ref, wu_ref, bu_ref, wd_ref, bd_ref, y_ref):
    i = pl.program_id(0)

    @pl.when(i < nused_ref[0])
    def _():
        xb = xs_ref[...].astype(BF16)
        d_ff = wd_ref.shape[1]
        acc = jnp.zeros(y_ref.shape, F32) + bd_ref[0]
        for c0 in range(0, d_ff, FF_CHUNK):
            up = _dot(xb, wu_ref[0, :, 2 * c0:2 * (c0 + FF_CHUNK)]) + bu_ref[0, :, 2 * c0:2 * (c0 + FF_CHUNK)]
            acts = []
            for g0 in range(0, 2 * FF_CHUNK, PAIR_GROUP):
                x_glu = jnp.minimum(up[:, g0:g0 + LANES], SWIGLU_LIMIT)
                x_lin = jnp.clip(up[:, g0 + LANES:g0 + PAIR_GROUP], -SWIGLU_LIMIT, SWIGLU_LIMIT)
                acts.append(x_glu * jax.nn.sigmoid(SWIGLU_ALPHA * x_glu) * (x_lin + 1.0))
            act = jnp.concatenate(acts, axis=1)
            acc = acc + _dot(act.astype(BF16), wd_ref[0, c0:c0 + FF_CHUNK, :])
        y_ref[...] = acc

    @pl.when(i >= nused_ref[0])
    def _():
        y_ref[...] = jnp.zeros_like(y_ref)


def _experts(block_e, n_used, xs, wu, bu, wd, bd):
    n_slots, d = xs.shape
    tm = EXPERT_ROWS
    d_ff = wd.shape[1]
    by_e3 = lambda i, be, nu: (be[i], 0, 0)
    grid_spec = pltpu.PrefetchScalarGridSpec(
        num_scalar_prefetch=2,
        grid=(n_slots // tm,),
        in_specs=[pl.BlockSpec((tm, d), lambda i, be, nu: (i, 0)),
                  pl.BlockSpec((1, d, 2 * d_ff), by_e3),
                  pl.BlockSpec((1, 1, 2 * d_ff), by_e3),
                  pl.BlockSpec((1, d_ff, d), by_e3),
                  pl.BlockSpec((1, 1, d), by_e3)],
        out_specs=pl.BlockSpec((tm, d), lambda i, be, nu: (i, 0)),
    )
    return pl.pallas_call(
        _expert_kernel,
        grid_spec=grid_spec,
        out_shape=jax.ShapeDtypeStruct((n_slots, d), F32),
        compiler_params=_cparams(("arbitrary",)),
        name="experts",
    )(block_e, n_used, xs, wu, bu, wd, bd)


def _combine_kernel(dest_ref, y_ref, h_ref, gate_ref, g2_ref, b2_ref, o_ref, buf_ref, sem, *, rows):
    base = pl.program_id(0) * rows

    def issue(r, _):
        for kk in range(TOP_K):
            _row_copy(y_ref, dest_ref[(base + r) * TOP_K + kk], buf_ref.at[kk], r, sem).start()
        return 0

    lax.fori_loop(0, rows, issue, 0)
    for kk in range(TOP_K):
        pltpu.make_async_copy(y_ref.at[pl.ds(0, rows)], buf_ref.at[kk], sem).wait()

    gates = gate_ref[...]
    ffn = jnp.zeros(o_ref.shape, F32)
    for kk in range(TOP_K):
        ffn = ffn + buf_ref[kk] * gates[:, kk:kk + 1]
    o_ref[...] = _layer_norm(DEEPNORM_ALPHA * h_ref[...] + ffn, g2_ref[...], b2_ref[...])


def _combine(dest_flat, y, h, gates, ln_g, ln_b):
    n, d = h.shape
    rows = COMBINE_ROWS
    grid_spec = pltpu.PrefetchScalarGridSpec(
        num_scalar_prefetch=1,
        grid=(n // rows,),
        in_specs=[pl.BlockSpec(memory_space=pl.ANY),
                  pl.BlockSpec((rows, d), lambda i, dest: (i, 0)),
                  pl.BlockSpec((rows, LANES), lambda i, dest: (i, 0)),
                  pl.BlockSpec(ln_g.shape, lambda i, dest: (0, 0)),
                  pl.BlockSpec(ln_b.shape, lambda i, dest: (0, 0))],
        out_specs=pl.BlockSpec((rows, d), lambda i, dest: (i, 0)),
        scratch_shapes=[pltpu.VMEM((TOP_K, rows, d), F32), pltpu.SemaphoreType.DMA],
    )
    return pl.pallas_call(
        functools.partial(_combine_kernel, rows=rows),
        grid_spec=grid_spec,
        out_shape=jax.ShapeDtypeStruct((n, d), F32),
        compiler_params=_cparams(("arbitrary",)),
        name="combine",
    )(dest_flat, y, h, gates, ln_g, ln_b)


def kernel(x, w_in, hgrn_lb_logits, hgrn_norm_g, w_branch_sb, w_branch_hgrn, w_out, ln1_g, ln1_b,
           router_w, router_b, expert_w_up, expert_b_up, expert_w_down, expert_b_down, ln2_g, ln2_b):
    b, t, d = x.shape
    assert w_in.shape[0] == DEPTH and hgrn_lb_logits.shape[0] == DEPTH + 1
    n = b * t
    sbw = w_branch_sb.shape[1]
    hgw = w_branch_hgrn.shape[1]
    widths = (sbw, sbw, sbw, hgw, hgw, hgw, hgw, d, d)
    assert sum(widths) == w_in.shape[2]
    n_exp, _, two_ff = expert_w_up.shape[1:]
    d_ff = two_ff // 2
    assert n_exp == N_EXPERTS and n % EXPERT_ROWS == 0

    x2 = x.reshape(n, d)
    sq, sk, sv, hq, hf, hi, hg, g_sb, g_hg = _inproj(x2, w_in[0].astype(BF16), widths)

    seq = lambda a: a.reshape(b, t, a.shape[-1])
    o_sb = _attention(seq(sq), seq(sk), seq(sv)).reshape(n, sbw)
    o_hg = _hgrn(seq(hq), seq(hf), seq(hi), seq(hg), hgrn_lb_logits, hgrn_norm_g).reshape(n, hgw)

    rw_pad = jnp.zeros((d, LANES), F32).at[:, :n_exp].set(router_w[0])
    rb_pad = jnp.full((1, LANES), NEG_BIG, F32).at[0, :n_exp].set(router_b[0])
    h, idx_slab, gate_slab, pos_slab, cnt = _mix(
        x2, o_sb, o_hg, g_sb, g_hg,
        w_branch_sb[0].astype(BF16), w_branch_hgrn[0].astype(BF16), w_out[0].astype(BF16),
        ln1_g, ln1_b, rw_pad, rb_pad)

    counts = cnt[0, :n_exp].astype(I32)
    padded = (counts + EXPERT_ROWS - 1) // EXPERT_ROWS * EXPERT_ROWS
    pad_end = jnp.cumsum(padded)
    pad_start = pad_end - padded
    top_idx = idx_slab[:, :TOP_K]
    dest = pad_start[top_idx] + pos_slab[:, :TOP_K]
    dest_flat = dest.reshape(-1).astype(I32)
    n_blocks = -(-(n * TOP_K + n_exp * (EXPERT_ROWS - 1)) // EXPERT_ROWS)
    n_slots = n_blocks * EXPERT_ROWS
    block_row0 = jnp.arange(n_blocks, dtype=I32) * EXPERT_ROWS
    block_e = jnp.minimum(
        jnp.sum((pad_end[None, :] <= block_row0[:, None]).astype(I32), axis=1), n_exp - 1)
    n_used = (pad_end[-1:] // EXPERT_ROWS).astype(I32)

    xs = _dispatch(dest_flat, h, n_slots)

    wu = _regroup_up_weights(expert_w_up[0])
    bu = expert_b_up[0].reshape(n_exp, two_ff // PAIR_GROUP, LANES, 2)
    bu = bu.transpose(0, 1, 3, 2).reshape(n_exp, 1, two_ff)
    wd = expert_w_down[0].astype(BF16)
    bd = expert_b_down[0][:, None, :]
    y = _experts(block_e, n_used, xs, wu, bu, wd, bd)

    out = _combine(dest_flat, y, h, gate_slab, ln2_g, ln2_b)
    return out.reshape(b, t, d)
```

```python
import functools
import math

import jax
import jax.numpy as jnp
import numpy as np
from jax import lax
from jax.experimental import pallas as pl
from jax.experimental.pallas import tpu as pltpu

F32 = jnp.float32
BF16 = jnp.bfloat16
I32 = jnp.int32

LANES = 128
SUBLANES = 8
PAIR_GROUP = 2 * LANES
VMEM_LIMIT_BYTES = 56 * 1024 * 1024

SB_HEAD_DIM = 64
HGRN_HEAD_DIM = 128
N_EXPERTS = 32
TOP_K = 4
SWIGLU_ALPHA = 1.702
SWIGLU_LIMIT = 7.0
DEPTH = 1
DEEPNORM_ALPHA = (2 * DEPTH) ** 0.25
LN_EPS = 1e-5
RMS_EPS = 1e-6

INPROJ_ROWS = 512
ATTN_BLOCK = 256
HGRN_CHUNK = 64
HGRN_STEP_ROWS = 256
HGRN_DIRECT = 8
MIX_ROWS = 256
EXPERT_ROWS = 256
FF_CHUNK = 512
COMBINE_ROWS = 256
NEG_BIG = -1e30
LOG2E = math.log2(math.e)


def _cparams(semantics):
    return pltpu.CompilerParams(dimension_semantics=semantics, vmem_limit_bytes=VMEM_LIMIT_BYTES)


def _split_bf16(a):
    hi = a.astype(BF16)
    lo = (a - hi.astype(F32)).astype(BF16)
    return hi, lo


def _split_trunc(a):
    bits = lax.bitcast_convert_type(a, jnp.uint32) & jnp.uint32(0xFFFF0000)
    hi = lax.bitcast_convert_type(bits, F32)
    return hi.astype(BF16), (a - hi).astype(BF16)


def _dot_nt(a, b):
    return lax.dot_general(a, b, (((1,), (1,)), ((), ())), preferred_element_type=F32)


def _dot_tn(a, b):
    return lax.dot_general(a, b, (((0,), (0,)), ((), ())), preferred_element_type=F32)


def _dot(a, b):
    return jnp.dot(a, b, preferred_element_type=F32)


def _inproj_kernel(x_ref, w_ref, *out_refs, widths):
    xb = x_ref[...].astype(BF16)
    off = 0
    for ref, width in zip(out_refs, widths):
        ref[...] = _dot(xb, w_ref[:, off:off + width]).astype(ref.dtype)
        off += width


def _inproj(x2, w_bf, widths):
    n, d = x2.shape
    tm = INPROJ_ROWS
    out_shape = [jax.ShapeDtypeStruct((n, w), BF16) for w in widths]
    out_specs = [pl.BlockSpec((tm, w), lambda i: (i, 0)) for w in widths]
    return pl.pallas_call(
        functools.partial(_inproj_kernel, widths=widths),
        grid=(n // tm,),
        in_specs=[pl.BlockSpec((tm, d), lambda i: (i, 0)),
                  pl.BlockSpec(w_bf.shape, lambda i: (0, 0))],
        out_specs=out_specs,
        out_shape=out_shape,
        compiler_params=_cparams(("arbitrary",)),
        name="inproj",
    )(x2, w_bf)


def _attn_kernel(q_ref, k_ref, v_ref, tri_ref, o_ref, *, blk, scale):
    qi = pl.program_id(2)
    q = q_ref[0].astype(F32) * scale
    lane = lax.broadcasted_iota(I32, q.shape, 1)
    first = lane < SB_HEAD_DIM
    q_heads = (jnp.where(first, q, 0.0).astype(BF16), jnp.where(first, 0.0, q).astype(BF16))
    tri2 = tri_ref[...]
    row = lax.broadcasted_iota(I32, (blk, blk), 0)
    col = lax.broadcasted_iota(I32, (blk, blk), 1)
    strictly_before = col < row
    heads = range(len(q_heads))

    def scores(kb):
        start = pl.multiple_of(kb * blk, blk)
        ks = k_ref[0, pl.ds(start, blk), :]
        return tuple(_dot_nt(qh, ks) for qh in q_heads)

    def finish(kb, zs, carry, diagonal):
        start = pl.multiple_of(kb * blk, blk)
        vs = v_ref[0, pl.ds(start, blk), :]
        ts = [jnp.log(1.0 + jnp.exp2(jnp.abs(z) * -LOG2E)) for z in zs]
        log_beta = [jnp.minimum(zs[h], 0.0) - ts[h] for h in heads]
        log_keep = [log_beta[h] - zs[h] for h in heads]
        if diagonal:
            log_keep = [jnp.where(strictly_before, lk, 0.0) for lk in log_keep]
        parts = [jnp.concatenate(_split_trunc(lk), axis=1) for lk in log_keep]
        log_after = [_dot(parts[h], tri2) + carry[h][1] for h in heads]
        ws = [jnp.exp(log_beta[h] + log_after[h]) for h in heads]
        if diagonal:
            ws = [jnp.where(strictly_before, w, 0.0) for w in ws]
        return tuple((carry[h][0] + _dot(ws[h].astype(BF16), vs),
                      carry[h][1] + jnp.sum(log_keep[h], axis=1, keepdims=True)) for h in heads)

    zero = (jnp.zeros((blk, LANES), F32), jnp.zeros((blk, 1), F32))
    carry = finish(qi, scores(qi), (zero, zero), True)

    def body(j, state):
        zs, carry = state
        kb = qi - 1 - j
        zs_next = scores(jnp.maximum(kb - 1, 0))
        return zs_next, finish(kb, zs, carry, False)

    _, carry = lax.fori_loop(0, qi, body, (scores(jnp.maximum(qi - 1, 0)), carry))
    o_ref[0] = jnp.where(first, carry[0][0], carry[1][0]).astype(o_ref.dtype)


def _attention(q, k, v):
    b, t, w = q.shape
    blk = ATTN_BLOCK
    pairs = w // LANES
    idx = np.arange(blk)
    tri = idx[:, None] > idx[None, :]
    tri = jnp.asarray(np.concatenate([tri, tri], axis=0), BF16)
    return pl.pallas_call(
        functools.partial(_attn_kernel, blk=blk, scale=SB_HEAD_DIM ** -0.5),
        grid=(b, pairs, t // blk),
        in_specs=[pl.BlockSpec((1, blk, LANES), lambda bi, hp, qi: (bi, qi, hp)),
                  pl.BlockSpec((1, t, LANES), lambda bi, hp, qi: (bi, 0, hp)),
                  pl.BlockSpec((1, t, LANES), lambda bi, hp, qi: (bi, 0, hp)),
                  pl.BlockSpec((2 * blk, blk), lambda bi, hp, qi: (0, 0))],
        out_specs=pl.BlockSpec((1, blk, LANES), lambda bi, hp, qi: (bi, qi, hp)),
        out_shape=jax.ShapeDtypeStruct((b, t, w), BF16),
        compiler_params=_cparams(("arbitrary", "arbitrary", "arbitrary")),
        name="attn",
    )(q, k, v, tri)


def _hgrn_level_sizes():
    sizes = []
    s = HGRN_CHUNK
    while s > HGRN_DIRECT:
        sizes.append(s)
        s //= 2
    return tuple(sizes)


def _hgrn_kernel(hq_ref, hf_ref, hi_ref, hg_ref, lbl_ref, ng_ref, tri_ref, mask_ref, o_ref, st_ref,
                 *, chunks):
    c = HGRN_CHUNK
    d = HGRN_DIRECT
    groups = c // d

    @pl.when(pl.program_id(2) == 0)
    def _():
        st_ref[...] = jnp.zeros_like(st_ref)

    logits = lbl_ref[...].astype(F32)
    ex = jnp.exp(logits - jnp.max(logits, axis=0, keepdims=True))
    lb = ex[0:1, :] / jnp.sum(ex, axis=0, keepdims=True)
    norm_g = ng_ref[...]
    tri = tri_ref[...]
    sub = lax.broadcasted_iota(I32, (groups, d, LANES), 1)

    for ci in range(chunks):
        rows = slice(ci * c, (ci + 1) * c)
        f = lb + (1.0 - lb) * jax.nn.sigmoid(hf_ref[0, rows, :].astype(F32))
        g = jnp.log(f)
        kk = 1.0 - f
        qq = jax.nn.silu(hq_ref[0, rows, :].astype(F32))
        vv = hi_ref[0, rows, :].astype(F32)
        vb = vv.astype(BF16)

        g_hi, g_lo = _split_bf16(g)
        bcum = _dot(tri, g_hi) + _dot(tri, g_lo)
        b_last = bcum[c - 1:c, :]

        st = st_ref[...]
        o = _dot_nt((qq * jnp.exp(bcum)).astype(BF16), st.astype(BF16))

        scores = jnp.zeros((c, c), F32)
        for li, size in enumerate(_hgrn_level_sizes()):
            half = size // 2
            ref_rows = jnp.concatenate(
                [jnp.broadcast_to(bcum[bi * size + half - 1:bi * size + half, :], (size, LANES))
                 for bi in range(c // size)], axis=0)
            q_dec = qq * jnp.exp(jnp.minimum(bcum - ref_rows, 0.0))
            k_dec = kk * jnp.exp(jnp.minimum(ref_rows - bcum, 0.0))
            sc = _dot_nt(q_dec.astype(BF16), k_dec.astype(BF16))
            scores = scores + jnp.where(mask_ref[li] > 0.0, sc, 0.0)
        o = o + _dot(scores.astype(BF16), vb)

        b3 = bcum.reshape(groups, d, LANES)
        q3 = qq.reshape(groups, d, LANES)
        k3 = kk.reshape(groups, d, LANES)
        v3 = vv.reshape(groups, d, LANES)
        od = jnp.zeros((groups, d, LANES), F32)
        for j in range(d):
            e = jnp.exp(jnp.minimum(b3 - b3[:, j:j + 1, :], 0.0))
            colj = jnp.sum(q3 * e * k3[:, j:j + 1, :], axis=-1, keepdims=True)
            colj = jnp.where(sub[:, :, 0:1] >= j, colj, 0.0)
            od = od + colj * v3[:, j:j + 1, :]
        o = o + od.reshape(c, LANES)

        k_dec = kk * jnp.exp(b_last - bcum)
        st_ref[...] = st * jnp.exp(b_last) + _dot_tn(vb, k_dec.astype(BF16))

        o = o * lax.rsqrt(jnp.mean(jnp.square(o), axis=-1, keepdims=True) + RMS_EPS)
        o = o * norm_g * jax.nn.sigmoid(hg_ref[0, rows, :].astype(F32))
        o_ref[0, rows, :] = o.astype(o_ref.dtype)


def _hgrn(hq, hf, hi, hg, lb_logits, norm_g):
    b, t, w = hq.shape
    heads = w // HGRN_HEAD_DIM
    c = HGRN_CHUNK
    rows = min(HGRN_STEP_ROWS, t)
    idx = np.arange(c)
    tri = jnp.asarray(idx[:, None] >= idx[None, :], BF16)
    masks = []
    for size in _hgrn_level_sizes():
        half = size // 2
        same = (idx[:, None] // size) == (idx[None, :] // size)
        masks.append(same & ((idx[:, None] % size) >= half) & ((idx[None, :] % size) < half))
    masks = jnp.asarray(np.stack(masks), F32)
    seq_spec = pl.BlockSpec((1, rows, LANES), lambda bi, h, ti: (bi, ti, h))
    return pl.pallas_call(
        functools.partial(_hgrn_kernel, chunks=rows // c),
        grid=(b, heads, t // rows),
        in_specs=[seq_spec, seq_spec, seq_spec, seq_spec,
                  pl.BlockSpec((lb_logits.shape[0], LANES), lambda bi, h, ti: (0, h)),
                  pl.BlockSpec((1, LANES), lambda bi, h, ti: (0, h)),
                  pl.BlockSpec((c, c), lambda bi, h, ti: (0, 0)),
                  pl.BlockSpec(masks.shape, lambda bi, h, ti: (0, 0, 0))],
        out_specs=seq_spec,
        out_shape=jax.ShapeDtypeStruct((b, t, w), BF16),
        scratch_shapes=[pltpu.VMEM((HGRN_HEAD_DIM, HGRN_HEAD_DIM), F32)],
        compiler_params=_cparams(("arbitrary", "arbitrary", "arbitrary")),
        name="hgrn",
    )(hq, hf, hi, hg, lb_logits, norm_g, tri, masks)


def _layer_norm(v, g, b):
    mu = jnp.mean(v, axis=-1, keepdims=True)
    cen = v - mu
    var = jnp.mean(jnp.square(cen), axis=-1, keepdims=True)
    return cen * lax.rsqrt(var + LN_EPS) * g + b


def _mix_kernel(x_ref, osb_ref, ohg_ref, gsb_ref, ghg_ref, wsb_ref, whg_ref, wout_ref,
                g1_ref, b1_ref, rw_ref, rb_ref, tri_ref,
                h_ref, idx_ref, gate_ref, pos_ref, cnt_ref, run_ref):
    @pl.when(pl.program_id(0) == 0)
    def _():
        run_ref[...] = jnp.zeros_like(run_ref)

    merged = (jax.nn.sigmoid(gsb_ref[...].astype(F32)) * _dot(osb_ref[...], wsb_ref[...])
              + jax.nn.sigmoid(ghg_ref[...].astype(F32)) * _dot(ohg_ref[...], whg_ref[...]))
    mix = _dot(merged.astype(BF16), wout_ref[...])
    h = _layer_norm(DEEPNORM_ALPHA * x_ref[...] + mix, g1_ref[...], b1_ref[...])
    h_ref[...] = h

    h_hi, h_lo = _split_bf16(h)
    h_lo2 = (h - h_hi.astype(F32) - h_lo.astype(F32)).astype(BF16)
    rw = rw_ref[...]
    rw_hi, rw_lo = _split_bf16(rw)
    rw_lo2 = (rw - rw_hi.astype(F32) - rw_lo.astype(F32)).astype(BF16)
    logits = (_dot(h_hi, rw_hi) + (_dot(h_hi, rw_lo) + _dot(h_lo, rw_hi))
              + (_dot(h_lo, rw_lo) + _dot(h_hi, rw_lo2) + _dot(h_lo2, rw_hi))) + rb_ref[...]

    tm = logits.shape[0]
    lane = lax.broadcasted_iota(I32, (tm, LANES), 1)
    remaining = logits
    vals, idxs, hots = [], [], []
    for _ in range(TOP_K):
        m = jnp.max(remaining, axis=-1, keepdims=True)
        sel = jnp.min(jnp.where(remaining == m, lane, LANES), axis=-1, keepdims=True)
        hot = lane == sel
        remaining = jnp.where(hot, NEG_BIG, remaining)
        vals.append(m)
        idxs.append(sel)
        hots.append(hot)
    exps = [jnp.exp(v - vals[0]) for v in vals]
    denom = exps[0] + exps[1] + exps[2] + exps[3]

    chosen = jnp.where(hots[0] | hots[1] | hots[2] | hots[3], 1.0, 0.0)
    before = _dot(tri_ref[...], chosen.astype(BF16)) + run_ref[0:1, :]
    idx_slab = jnp.zeros((tm, LANES), I32)
    gate_slab = jnp.zeros((tm, LANES), F32)
    pos_slab = jnp.zeros((tm, LANES), I32)
    for kk in range(TOP_K):
        rank = jnp.sum(jnp.where(hots[kk], before, 0.0), axis=-1, keepdims=True)
        here = lane == kk
        idx_slab = jnp.where(here, idxs[kk], idx_slab)
        gate_slab = jnp.where(here, exps[kk] / denom, gate_slab)
        pos_slab = jnp.where(here, rank.astype(I32), pos_slab)
    idx_ref[...] = idx_slab
    gate_ref[...] = gate_slab
    pos_ref[...] = pos_slab
    run_ref[0:1, :] = run_ref[0:1, :] + jnp.sum(chosen, axis=0, keepdims=True)
    cnt_ref[...] = run_ref[...]


def _mix(x2, o_sb, o_hg, g_sb, g_hg, wsb, whg, wout, ln_g, ln_b, rw_pad, rb_pad):
    n, d = x2.shape
    tm = MIX_ROWS
    idx = np.arange(tm)
    tri = jnp.asarray(idx[:, None] > idx[None, :], BF16)
    row = lambda w: pl.BlockSpec((tm, w), lambda i: (i, 0))
    full = lambda a: pl.BlockSpec(a.shape, lambda i: (0,) * a.ndim)
    return pl.pallas_call(
        _mix_kernel,
        grid=(n // tm,),
        in_specs=[row(d), row(o_sb.shape[1]), row(o_hg.shape[1]), row(d), row(d),
                  full(wsb), full(whg), full(wout), full(ln_g), full(ln_b),
                  full(rw_pad), full(rb_pad), full(tri)],
        out_specs=[row(d), row(LANES), row(LANES), row(LANES),
                   pl.BlockSpec((SUBLANES, LANES), lambda i: (0, 0))],
        out_shape=[jax.ShapeDtypeStruct((n, d), F32),
                   jax.ShapeDtypeStruct((n, LANES), I32),
                   jax.ShapeDtypeStruct((n, LANES), F32),
                   jax.ShapeDtypeStruct((n, LANES), I32),
                   jax.ShapeDtypeStruct((SUBLANES, LANES), F32)],
        scratch_shapes=[pltpu.VMEM((SUBLANES, LANES), F32)],
        compiler_params=_cparams(("arbitrary",)),
        name="mix",
    )(x2, o_sb, o_hg, g_sb, g_hg, wsb, whg, wout, ln_g, ln_b, rw_pad, rb_pad, tri)


def _pair_split_perm():
    p = np.zeros((PAIR_GROUP, PAIR_GROUP), np.float32)
    i = np.arange(LANES)
    p[2 * i, i] = 1.0
    p[2 * i + 1, LANES + i] = 1.0
    return p


def _regroup_kernel(w_ref, perm_ref, o_ref):
    perm = perm_ref[...]
    for c0 in range(0, w_ref.shape[2], PAIR_GROUP):
        cols = slice(c0, c0 + PAIR_GROUP)
        o_ref[0, :, cols] = _dot(w_ref[0, :, cols].astype(BF16), perm).astype(o_ref.dtype)


def _regroup_up_weights(w_up):
    n_exp, d, two_ff = w_up.shape
    cols = 4 * PAIR_GROUP
    perm = jnp.asarray(_pair_split_perm(), BF16)
    return pl.pallas_call(
        _regroup_kernel,
        grid=(n_exp, two_ff // cols),
        in_specs=[pl.BlockSpec((1, d, cols), lambda e, c: (e, 0, c)),
                  pl.BlockSpec(perm.shape, lambda e, c: (0, 0))],
        out_specs=pl.BlockSpec((1, d, cols), lambda e, c: (e, 0, c)),
        out_shape=jax.ShapeDtypeStruct(w_up.shape, BF16),
        compiler_params=_cparams(("arbitrary", "arbitrary")),
        name="regroup",
    )(w_up, perm)


def _expert_kernel(be_ref, nused_ref, tok_next_ref, tok_first_ref, dst_prev_ref, dst_last_ref,
                   h_ref, wu_ref, bu_ref, wd_ref, bd_ref, yk_ref,
                   xbuf, ybuf, gsem, ssem, *, dump_row0):
    i = pl.program_id(0)
    last = pl.num_programs(0) - 1
    slot = i % 2
    other = 1 - slot
    tm = xbuf.shape[1]
    d_ff = wd_ref.shape[1]

    def gather(idx_ref, r, s):
        return pltpu.make_async_copy(h_ref.at[pl.ds(idx_ref[0, 0, r], 1)],
                                     xbuf.at[s, pl.ds(r, 1)], gsem.at[s])

    def scatter(idx_ref, r, s):
        return pltpu.make_async_copy(ybuf.at[s, pl.ds(r, 1)],
                                     yk_ref.at[pl.ds(idx_ref[0, 0, r], 1)], ssem.at[s])

    def wait_gathers(s):
        pltpu.make_async_copy(h_ref.at[pl.ds(0, tm)], xbuf.at[s], gsem.at[s]).wait()

    def wait_scatters(s):
        pltpu.make_async_copy(ybuf.at[s], yk_ref.at[pl.ds(0, tm)], ssem.at[s]).wait()

    def issue_rows(r0, r1):
        for r in range(r0, r1):
            gather(tok_next_ref, r, other).start()
            scatter(dst_prev_ref, r, other).start()

    @pl.when(i == 0)
    def _():
        ybuf[...] = jnp.zeros_like(ybuf)
        for s in range(2):
            fill = pltpu.make_async_copy(ybuf.at[s], yk_ref.at[pl.ds(dump_row0 + s * tm, tm)],
                                         ssem.at[s])
            fill.start()
            fill.wait()

        def first(r, _):
            gather(tok_first_ref, r, 0).start()
            return 0
        lax.fori_loop(0, tm, first, 0)

    wait_gathers(slot)

    @pl.when(i > 0)
    def _():
        wait_scatters(slot)

    @pl.when(i < nused_ref[0])
    def _():
        xb = xbuf[slot].astype(BF16)
        acc = jnp.zeros((tm, wd_ref.shape[2]), F32) + bd_ref[0]
        n_chunks = d_ff // FF_CHUNK
        per = tm // (2 * n_chunks)
        for ci in range(n_chunks):
            c0 = ci * FF_CHUNK
            issue_rows(2 * ci * per, (2 * ci + 1) * per)
            up = _dot(xb, wu_ref[0, :, 2 * c0:2 * (c0 + FF_CHUNK)]) + bu_ref[0, :, 2 * c0:2 * (c0 + FF_CHUNK)]
            acts = []
            for g0 in range(0, 2 * FF_CHUNK, PAIR_GROUP):
                x_glu = jnp.minimum(up[:, g0:g0 + LANES], SWIGLU_LIMIT)
                x_lin = jnp.clip(up[:, g0 + LANES:g0 + PAIR_GROUP], -SWIGLU_LIMIT, SWIGLU_LIMIT)
                acts.append(x_glu * jax.nn.sigmoid(SWIGLU_ALPHA * x_glu) * (x_lin + 1.0))
            act = jnp.concatenate(acts, axis=1)
            issue_rows((2 * ci + 1) * per, (2 * ci + 2) * per)
            acc = acc + _dot(act.astype(BF16), wd_ref[0, c0:c0 + FF_CHUNK, :])
        ybuf[slot] = acc

    @pl.when(i >= nused_ref[0])
    def _():
        def both(r, _):
            gather(tok_next_ref, r, other).start()
            scatter(dst_prev_ref, r, other).start()
            return 0
        lax.fori_loop(0, tm, both, 0)
        ybuf[slot] = jnp.zeros((tm, wd_ref.shape[2]), F32)

    @pl.when(i == last)
    def _():
        def tail(r, _):
            scatter(dst_last_ref, r, slot).start()
            return 0
        lax.fori_loop(0, tm, tail, 0)
        wait_scatters(other)
        wait_scatters(slot)
        wait_gathers(other)


def _experts(block_e, n_used, tok_blocks, dst_blocks, h, wu, bu, wd, bd):
    n, d = h.shape
    tm = EXPERT_ROWS
    n_blocks = tok_blocks.shape[0] - 2
    d_ff = wd.shape[1]
    by_e3 = lambda i, be, nu: (be[i], 0, 0)
    idx_spec = lambda f: pl.BlockSpec((1, 1, tm), f, memory_space=pltpu.SMEM)
    dump_row0 = TOP_K * n
    grid_spec = pltpu.PrefetchScalarGridSpec(
        num_scalar_prefetch=2,
        grid=(n_blocks,),
        in_specs=[idx_spec(lambda i, be, nu: (i + 2, 0, 0)),
                  idx_spec(lambda i, be, nu: (1, 0, 0)),
                  idx_spec(lambda i, be, nu: (i, 0, 0)),
                  idx_spec(lambda i, be, nu: (i + 1, 0, 0)),
                  pl.BlockSpec(memory_space=pl.ANY),
                  pl.BlockSpec((1, d, 2 * d_ff), by_e3),
                  pl.BlockSpec((1, 1, 2 * d_ff), by_e3),
                  pl.BlockSpec((1, d_ff, d), by_e3),
                  pl.BlockSpec((1, 1, d), by_e3)],
        out_specs=pl.BlockSpec(memory_space=pl.ANY),
        scratch_shapes=[pltpu.VMEM((2, tm, d), F32), pltpu.VMEM((2, tm, d), F32),
                        pltpu.SemaphoreType.DMA((2,)), pltpu.SemaphoreType.DMA((2,))],
    )
    return pl.pallas_call(
        functools.partial(_expert_kernel, dump_row0=dump_row0),
        grid_spec=grid_spec,
        out_shape=jax.ShapeDtypeStruct((dump_row0 + 2 * tm, d), F32),
        compiler_params=_cparams(("arbitrary",)),
        name="experts",
    )(block_e, n_used, tok_blocks, tok_blocks, dst_blocks, dst_blocks, h, wu, bu, wd, bd)


def _combine_kernel(y0_ref, y1_ref, y2_ref, y3_ref, h_ref, gate_ref, g2_ref, b2_ref, o_ref):
    gates = gate_ref[...]
    ffn = y0_ref[...] * gates[:, 0:1]
    for kk, y_ref in enumerate((y1_ref, y2_ref, y3_ref), start=1):
        ffn = ffn + y_ref[...] * gates[:, kk:kk + 1]
    o_ref[...] = _layer_norm(DEEPNORM_ALPHA * h_ref[...] + ffn, g2_ref[...], b2_ref[...])


def _combine(yk, h, gates, ln_g, ln_b):
    n, d = h.shape
    rows = COMBINE_ROWS
    per_choice = n // rows
    choice = lambda kk: pl.BlockSpec((rows, d), lambda i: (kk * per_choice + i, 0))
    return pl.pallas_call(
        _combine_kernel,
        grid=(per_choice,),
        in_specs=[choice(0), choice(1), choice(2), choice(3),
                  pl.BlockSpec((rows, d), lambda i: (i, 0)),
                  pl.BlockSpec((rows, LANES), lambda i: (i, 0)),
                  pl.BlockSpec(ln_g.shape, lambda i: (0, 0)),
                  pl.BlockSpec(ln_b.shape, lambda i: (0, 0))],
        out_specs=pl.BlockSpec((rows, d), lambda i: (i, 0)),
        out_shape=jax.ShapeDtypeStruct((n, d), F32),
        compiler_params=_cparams(("arbitrary",)),
        name="combine",
    )(yk, yk, yk, yk, h, gates, ln_g, ln_b)


def _slot_tables(idx_slab, pos_slab, cnt, n, n_exp):
    tm = EXPERT_ROWS
    n_asg = n * TOP_K
    counts = cnt[0, :n_exp].astype(I32)
    padded = (counts + tm - 1) // tm * tm
    pad_end = jnp.cumsum(padded)
    pad_start = pad_end - padded
    start = jnp.cumsum(counts) - counts
    dest = (pad_start[idx_slab[:, :TOP_K]] + pos_slab[:, :TOP_K]).reshape(-1).astype(I32)
    n_blocks = -(-(n_asg + n_exp * (tm - 1)) // tm)
    block_row0 = jnp.arange(n_blocks, dtype=I32) * tm
    block_e = jnp.minimum(
        jnp.sum((pad_end[None, :] <= block_row0[:, None]).astype(I32), axis=1), n_exp - 1)
    n_used = (pad_end[-1:] // tm).astype(I32)

    _, order = lax.sort_key_val(dest, jnp.arange(n_asg, dtype=I32))
    slots = jnp.arange(n_blocks * tm, dtype=I32)
    e_slot = jnp.repeat(block_e, tm)
    offset = slots - pad_start[e_slot]
    valid = offset < counts[e_slot]
    asg = order[jnp.clip(start[e_slot] + offset, 0, n_asg - 1)]
    tok = jnp.where(valid, asg // TOP_K, 0)
    dst = jnp.where(valid, (asg % TOP_K) * n + tok, n_asg + slots % (2 * tm))
    lead = jnp.arange(tm, dtype=I32)
    tok_blocks = jnp.concatenate([jnp.zeros((tm,), I32), tok, jnp.zeros((tm,), I32)])
    dst_blocks = jnp.concatenate([n_asg + tm + lead, dst, n_asg + lead])
    shape = (n_blocks + 2, 1, tm)
    return block_e, n_used, tok_blocks.reshape(shape), dst_blocks.reshape(shape)


def kernel(x, w_in, hgrn_lb_logits, hgrn_norm_g, w_branch_sb, w_branch_hgrn, w_out, ln1_g, ln1_b,
           router_w, router_b, expert_w_up, expert_b_up, expert_w_down, expert_b_down, ln2_g, ln2_b):
    b, t, d = x.shape
    assert w_in.shape[0] == DEPTH and hgrn_lb_logits.shape[0] == DEPTH + 1
    n = b * t
    sbw = w_branch_sb.shape[1]
    hgw = w_branch_hgrn.shape[1]
    widths = (sbw, sbw, sbw, hgw, hgw, hgw, hgw, d, d)
    assert sum(widths) == w_in.shape[2]
    n_exp, _, two_ff = expert_w_up.shape[1:]
    assert n_exp == N_EXPERTS

    x2 = x.reshape(n, d)
    sq, sk, sv, hq, hf, hi, hg, g_sb, g_hg = _inproj(x2, w_in[0].astype(BF16), widths)

    seq = lambda a: a.reshape(b, t, a.shape[-1])
    o_sb = _attention(seq(sq), seq(sk), seq(sv)).reshape(n, sbw)
    o_hg = _hgrn(seq(hq), seq(hf), seq(hi), seq(hg), hgrn_lb_logits, hgrn_norm_g).reshape(n, hgw)

    rw_pad = jnp.zeros((d, LANES), F32).at[:, :n_exp].set(router_w[0])
    rb_pad = jnp.full((1, LANES), NEG_BIG, F32).at[0, :n_exp].set(router_b[0])
    h, idx_slab, gate_slab, pos_slab, cnt = _mix(
        x2, o_sb, o_hg, g_sb, g_hg,
        w_branch_sb[0].astype(BF16), w_branch_hgrn[0].astype(BF16), w_out[0].astype(BF16),
        ln1_g, ln1_b, rw_pad, rb_pad)

    block_e, n_used, tok_blocks, dst_blocks = _slot_tables(idx_slab, pos_slab, cnt, n, n_exp)

    wu = _regroup_up_weights(expert_w_up[0])
    bu = expert_b_up[0].reshape(n_exp, two_ff // PAIR_GROUP, LANES, 2)
    bu = bu.transpose(0, 1, 3, 2).reshape(n_exp, 1, two_ff)
    wd = expert_w_down[0].astype(BF16)
    bd = expert_b_down[0][:, None, :]
    yk = _experts(block_e, n_used, tok_blocks, dst_blocks, h, wu, bu, wd, bd)

    out = _combine(yk, h, gate_slab, ln2_g, ln2_b)
    return out.reshape(b, t, d)
```

```python
import functools
import math

import jax
import jax.numpy as jnp
import numpy as np
from jax import lax
from jax.experimental import pallas as pl
from jax.experimental.pallas import tpu as pltpu

F32 = jnp.float32
BF16 = jnp.bfloat16
I32 = jnp.int32

LANES = 128
SUBLANES = 8
PAIR_GROUP = 2 * LANES
VMEM_LIMIT_BYTES = 56 * 1024 * 1024

SB_HEAD_DIM = 64
HGRN_HEAD_DIM = 128
N_EXPERTS = 32
TOP_K = 4
SWIGLU_ALPHA = 1.702
SWIGLU_LIMIT = 7.0
DEPTH = 1
DEEPNORM_ALPHA = (2 * DEPTH) ** 0.25
LN_EPS = 1e-5
RMS_EPS = 1e-6

INPROJ_ROWS = 512
ATTN_BLOCK = 256
HGRN_CHUNK = 64
HGRN_STEP_ROWS = 256
HGRN_DIRECT = 8
MIX_ROWS = 256
EXPERT_ROWS = 256
X_SLOTS = 3
Y_SLOTS = 2
ISSUE_UNROLL = 8
FF_CHUNK = 512
COMBINE_ROWS = 256
NEG_BIG = -1e30
LOG2E = math.log2(math.e)


def _cparams(semantics):
    return pltpu.CompilerParams(dimension_semantics=semantics, vmem_limit_bytes=VMEM_LIMIT_BYTES)


def _split_bf16(a):
    hi = a.astype(BF16)
    lo = (a - hi.astype(F32)).astype(BF16)
    return hi, lo


def _split_trunc(a):
    bits = lax.bitcast_convert_type(a, jnp.uint32) & jnp.uint32(0xFFFF0000)
    hi = lax.bitcast_convert_type(bits, F32)
    return hi.astype(BF16), (a - hi).astype(BF16)


def _dot_nt(a, b):
    return lax.dot_general(a, b, (((1,), (1,)), ((), ())), preferred_element_type=F32)


def _dot_tn(a, b):
    return lax.dot_general(a, b, (((0,), (0,)), ((), ())), preferred_element_type=F32)


def _dot(a, b):
    return jnp.dot(a, b, preferred_element_type=F32)


def _inproj_kernel(x_ref, w_ref, *out_refs, widths):
    xb = x_ref[...].astype(BF16)
    off = 0
    for ref, width in zip(out_refs, widths):
        ref[...] = _dot(xb, w_ref[:, off:off + width]).astype(ref.dtype)
        off += width


def _inproj(x2, w_bf, widths):
    n, d = x2.shape
    tm = INPROJ_ROWS
    out_shape = [jax.ShapeDtypeStruct((n, w), BF16) for w in widths]
    out_specs = [pl.BlockSpec((tm, w), lambda i: (i, 0)) for w in widths]
    return pl.pallas_call(
        functools.partial(_inproj_kernel, widths=widths),
        grid=(n // tm,),
        in_specs=[pl.BlockSpec((tm, d), lambda i: (i, 0)),
                  pl.BlockSpec(w_bf.shape, lambda i: (0, 0))],
        out_specs=out_specs,
        out_shape=out_shape,
        compiler_params=_cparams(("arbitrary",)),
        name="inproj",
    )(x2, w_bf)


def _attn_kernel(q_ref, k_ref, v_ref, tri_ref, o_ref, *, blk, scale):
    qi = pl.program_id(2)
    q = q_ref[0].astype(F32) * scale
    lane = lax.broadcasted_iota(I32, q.shape, 1)
    first = lane < SB_HEAD_DIM
    q_heads = (jnp.where(first, q, 0.0).astype(BF16), jnp.where(first, 0.0, q).astype(BF16))
    tri2 = tri_ref[...]
    row = lax.broadcasted_iota(I32, (blk, blk), 0)
    col = lax.broadcasted_iota(I32, (blk, blk), 1)
    strictly_before = col < row
    heads = range(len(q_heads))

    def scores(kb):
        start = pl.multiple_of(kb * blk, blk)
        ks = k_ref[0, pl.ds(start, blk), :]
        return tuple(_dot_nt(qh, ks) for qh in q_heads)

    def finish(kb, zs, carry, diagonal):
        start = pl.multiple_of(kb * blk, blk)
        vs = v_ref[0, pl.ds(start, blk), :]
        ts = [jnp.log(1.0 + jnp.exp2(jnp.abs(z) * -LOG2E)) for z in zs]
        log_beta = [jnp.minimum(zs[h], 0.0) - ts[h] for h in heads]
        log_keep = [log_beta[h] - zs[h] for h in heads]
        if diagonal:
            log_keep = [jnp.where(strictly_before, lk, 0.0) for lk in log_keep]
        parts = [jnp.concatenate(_split_trunc(lk), axis=1) for lk in log_keep]
        log_after = [_dot(parts[h], tri2) + carry[h][1] for h in heads]
        ws = [jnp.exp(log_beta[h] + log_after[h]) for h in heads]
        if diagonal:
            ws = [jnp.where(strictly_before, w, 0.0) for w in ws]
        return tuple((carry[h][0] + _dot(ws[h].astype(BF16), vs),
                      carry[h][1] + jnp.sum(log_keep[h], axis=1, keepdims=True)) for h in heads)

    zero = (jnp.zeros((blk, LANES), F32), jnp.zeros((blk, 1), F32))
    carry = finish(qi, scores(qi), (zero, zero), True)

    def body(j, state):
        zs, carry = state
        kb = qi - 1 - j
        zs_next = scores(jnp.maximum(kb - 1, 0))
        return zs_next, finish(kb, zs, carry, False)

    _, carry = lax.fori_loop(0, qi, body, (scores(jnp.maximum(qi - 1, 0)), carry))
    o_ref[0] = jnp.where(first, carry[0][0], carry[1][0]).astype(o_ref.dtype)


def _attention(q, k, v):
    b, t, w = q.shape
    blk = ATTN_BLOCK
    pairs = w // LANES
    idx = np.arange(blk)
    tri = idx[:, None] > idx[None, :]
    tri = jnp.asarray(np.concatenate([tri, tri], axis=0), BF16)
    return pl.pallas_call(
        functools.partial(_attn_kernel, blk=blk, scale=SB_HEAD_DIM ** -0.5),
        grid=(b, pairs, t // blk),
        in_specs=[pl.BlockSpec((1, blk, LANES), lambda bi, hp, qi: (bi, qi, hp)),
                  pl.BlockSpec((1, t, LANES), lambda bi, hp, qi: (bi, 0, hp)),
                  pl.BlockSpec((1, t, LANES), lambda bi, hp, qi: (bi, 0, hp)),
                  pl.BlockSpec((2 * blk, blk), lambda bi, hp, qi: (0, 0))],
        out_specs=pl.BlockSpec((1, blk, LANES), lambda bi, hp, qi: (bi, qi, hp)),
        out_shape=jax.ShapeDtypeStruct((b, t, w), BF16),
        compiler_params=_cparams(("arbitrary", "arbitrary", "arbitrary")),
        name="attn",
    )(q, k, v, tri)


def _hgrn_level_sizes():
    sizes = []
    s = HGRN_CHUNK
    while s > HGRN_DIRECT:
        sizes.append(s)
        s //= 2
    return tuple(sizes)


def _hgrn_kernel(hq_ref, hf_ref, hi_ref, hg_ref, lbl_ref, ng_ref, tri_ref, mask_ref, o_ref, st_ref,
                 *, chunks):
    c = HGRN_CHUNK
    d = HGRN_DIRECT
    groups = c // d

    @pl.when(pl.program_id(2) == 0)
    def _():
        st_ref[...] = jnp.zeros_like(st_ref)

    logits = lbl_ref[...].astype(F32)
    ex = jnp.exp(logits - jnp.max(logits, axis=0, keepdims=True))
    lb = ex[0:1, :] / jnp.sum(ex, axis=0, keepdims=True)
    norm_g = ng_ref[...]
    tri = tri_ref[...]
    sub = lax.broadcasted_iota(I32, (groups, d, LANES), 1)

    for ci in range(chunks):
        rows = slice(ci * c, (ci + 1) * c)
        f = lb + (1.0 - lb) * jax.nn.sigmoid(hf_ref[0, rows, :].astype(F32))
        g = jnp.log(f)
        kk = 1.0 - f
        qq = jax.nn.silu(hq_ref[0, rows, :].astype(F32))
        vv = hi_ref[0, rows, :].astype(F32)
        vb = vv.astype(BF16)

        g_hi, g_lo = _split_bf16(g)
        bcum = _dot(tri, g_hi) + _dot(tri, g_lo)
        b_last = bcum[c - 1:c, :]

        st = st_ref[...]
        o = _dot_nt((qq * jnp.exp(bcum)).astype(BF16), st.astype(BF16))

        scores = jnp.zeros((c, c), F32)
        for li, size in enumerate(_hgrn_level_sizes()):
            half = size // 2
            ref_rows = jnp.concatenate(
                [jnp.broadcast_to(bcum[bi * size + half - 1:bi * size + half, :], (size, LANES))
                 for bi in range(c // size)], axis=0)
            q_dec = qq * jnp.exp(jnp.minimum(bcum - ref_rows, 0.0))
            k_dec = kk * jnp.exp(jnp.minimum(ref_rows - bcum, 0.0))
            sc = _dot_nt(q_dec.astype(BF16), k_dec.astype(BF16))
            scores = scores + jnp.where(mask_ref[li] > 0.0, sc, 0.0)
        o = o + _dot(scores.astype(BF16), vb)

        b3 = bcum.reshape(groups, d, LANES)
        q3 = qq.reshape(groups, d, LANES)
        k3 = kk.reshape(groups, d, LANES)
        v3 = vv.reshape(groups, d, LANES)
        od = jnp.zeros((groups, d, LANES), F32)
        for j in range(d):
            e = jnp.exp(jnp.minimum(b3 - b3[:, j:j + 1, :], 0.0))
            colj = jnp.sum(q3 * e * k3[:, j:j + 1, :], axis=-1, keepdims=True)
            colj = jnp.where(sub[:, :, 0:1] >= j, colj, 0.0)
            od = od + colj * v3[:, j:j + 1, :]
        o = o + od.reshape(c, LANES)

        k_dec = kk * jnp.exp(b_last - bcum)
        st_ref[...] = st * jnp.exp(b_last) + _dot_tn(vb, k_dec.astype(BF16))

        o = o * lax.rsqrt(jnp.mean(jnp.square(o), axis=-1, keepdims=True) + RMS_EPS)
        o = o * norm_g * jax.nn.sigmoid(hg_ref[0, rows, :].astype(F32))
        o_ref[0, rows, :] = o.astype(o_ref.dtype)


def _hgrn(hq, hf, hi, hg, lb_logits, norm_g):
    b, t, w = hq.shape
    heads = w // HGRN_HEAD_DIM
    c = HGRN_CHUNK
    rows = min(HGRN_STEP_ROWS, t)
    idx = np.arange(c)
    tri = jnp.asarray(idx[:, None] >= idx[None, :], BF16)
    masks = []
    for size in _hgrn_level_sizes():
        half = size // 2
        same = (idx[:, None] // size) == (idx[None, :] // size)
        masks.append(same & ((idx[:, None] % size) >= half) & ((idx[None, :] % size) < half))
    masks = jnp.asarray(np.stack(masks), F32)
    seq_spec = pl.BlockSpec((1, rows, LANES), lambda bi, h, ti: (bi, ti, h))
    return pl.pallas_call(
        functools.partial(_hgrn_kernel, chunks=rows // c),
        grid=(b, heads, t // rows),
        in_specs=[seq_spec, seq_spec, seq_spec, seq_spec,
                  pl.BlockSpec((lb_logits.shape[0], LANES), lambda bi, h, ti: (0, h)),
                  pl.BlockSpec((1, LANES), lambda bi, h, ti: (0, h)),
                  pl.BlockSpec((c, c), lambda bi, h, ti: (0, 0)),
                  pl.BlockSpec(masks.shape, lambda bi, h, ti: (0, 0, 0))],
        out_specs=seq_spec,
        out_shape=jax.ShapeDtypeStruct((b, t, w), BF16),
        scratch_shapes=[pltpu.VMEM((HGRN_HEAD_DIM, HGRN_HEAD_DIM), F32)],
        compiler_params=_cparams(("arbitrary", "arbitrary", "arbitrary")),
        name="hgrn",
    )(hq, hf, hi, hg, lb_logits, norm_g, tri, masks)


def _store_row_tiles(ref, row0, value):
    rows, width = value.shape
    per_row = width // LANES
    for c in range(per_row):
        ref[pl.ds(row0 * per_row + c, rows, stride=per_row), :] = value[:, c * LANES:(c + 1) * LANES]


def _load_row_tiles(ref, row0, rows, per_row):
    return [ref[pl.ds(row0 * per_row + c, rows, stride=per_row), :] for c in range(per_row)]


def _layer_norm(v, g, b):
    mu = jnp.mean(v, axis=-1, keepdims=True)
    cen = v - mu
    var = jnp.mean(jnp.square(cen), axis=-1, keepdims=True)
    return cen * lax.rsqrt(var + LN_EPS) * g + b


def _mix_kernel(x_ref, osb_ref, ohg_ref, gsb_ref, ghg_ref, wsb_ref, whg_ref, wout_ref,
                g1_ref, b1_ref, rw_ref, rb_ref, tri_ref,
                h_ref, idx_ref, gate_ref, pos_ref, cnt_ref, run_ref):
    @pl.when(pl.program_id(0) == 0)
    def _():
        run_ref[...] = jnp.zeros_like(run_ref)

    merged = (jax.nn.sigmoid(gsb_ref[...].astype(F32)) * _dot(osb_ref[...], wsb_ref[...])
              + jax.nn.sigmoid(ghg_ref[...].astype(F32)) * _dot(ohg_ref[...], whg_ref[...]))
    mix = _dot(merged.astype(BF16), wout_ref[...])
    h = _layer_norm(DEEPNORM_ALPHA * x_ref[...] + mix, g1_ref[...], b1_ref[...])
    _store_row_tiles(h_ref, 0, h)

    h_hi, h_lo = _split_bf16(h)
    h_lo2 = (h - h_hi.astype(F32) - h_lo.astype(F32)).astype(BF16)
    rw = rw_ref[...]
    rw_hi, rw_lo = _split_bf16(rw)
    rw_lo2 = (rw - rw_hi.astype(F32) - rw_lo.astype(F32)).astype(BF16)
    logits = (_dot(h_hi, rw_hi) + (_dot(h_hi, rw_lo) + _dot(h_lo, rw_hi))
              + (_dot(h_lo, rw_lo) + _dot(h_hi, rw_lo2) + _dot(h_lo2, rw_hi))) + rb_ref[...]

    tm = logits.shape[0]
    lane = lax.broadcasted_iota(I32, (tm, LANES), 1)
    remaining = logits
    vals, idxs, hots = [], [], []
    for _ in range(TOP_K):
        m = jnp.max(remaining, axis=-1, keepdims=True)
        sel = jnp.min(jnp.where(remaining == m, lane, LANES), axis=-1, keepdims=True)
        hot = lane == sel
        remaining = jnp.where(hot, NEG_BIG, remaining)
        vals.append(m)
        idxs.append(sel)
        hots.append(hot)
    exps = [jnp.exp(v - vals[0]) for v in vals]
    denom = exps[0] + exps[1] + exps[2] + exps[3]

    chosen = jnp.where(hots[0] | hots[1] | hots[2] | hots[3], 1.0, 0.0)
    before = _dot(tri_ref[...], chosen.astype(BF16)) + run_ref[0:1, :]
    idx_slab = jnp.zeros((tm, LANES), I32)
    gate_slab = jnp.zeros((tm, LANES), F32)
    pos_slab = jnp.zeros((tm, LANES), I32)
    for kk in range(TOP_K):
        rank = jnp.sum(jnp.where(hots[kk], before, 0.0), axis=-1, keepdims=True)
        here = lane == kk
        idx_slab = jnp.where(here, idxs[kk], idx_slab)
        gate_slab = jnp.where(here, exps[kk] / denom, gate_slab)
        pos_slab = jnp.where(here, rank.astype(I32), pos_slab)
    idx_ref[...] = idx_slab
    gate_ref[...] = gate_slab
    pos_ref[...] = pos_slab
    run_ref[0:1, :] = run_ref[0:1, :] + jnp.sum(chosen, axis=0, keepdims=True)
    cnt_ref[...] = run_ref[...]


def _mix(x2, o_sb, o_hg, g_sb, g_hg, wsb, whg, wout, ln_g, ln_b, rw_pad, rb_pad):
    n, d = x2.shape
    tm = MIX_ROWS
    idx = np.arange(tm)
    tri = jnp.asarray(idx[:, None] > idx[None, :], BF16)
    row = lambda w: pl.BlockSpec((tm, w), lambda i: (i, 0))
    full = lambda a: pl.BlockSpec(a.shape, lambda i: (0,) * a.ndim)
    return pl.pallas_call(
        _mix_kernel,
        grid=(n // tm,),
        in_specs=[row(d), row(o_sb.shape[1]), row(o_hg.shape[1]), row(d), row(d),
                  full(wsb), full(whg), full(wout), full(ln_g), full(ln_b),
                  full(rw_pad), full(rb_pad), full(tri)],
        out_specs=[pl.BlockSpec((tm * d // LANES, LANES), lambda i: (i, 0)),
                   row(LANES), row(LANES), row(LANES),
                   pl.BlockSpec((SUBLANES, LANES), lambda i: (0, 0))],
        out_shape=[jax.ShapeDtypeStruct((n * d // LANES, LANES), F32),
                   jax.ShapeDtypeStruct((n, LANES), I32),
                   jax.ShapeDtypeStruct((n, LANES), F32),
                   jax.ShapeDtypeStruct((n, LANES), I32),
                   jax.ShapeDtypeStruct((SUBLANES, LANES), F32)],
        scratch_shapes=[pltpu.VMEM((SUBLANES, LANES), F32)],
        compiler_params=_cparams(("arbitrary",)),
        name="mix",
    )(x2, o_sb, o_hg, g_sb, g_hg, wsb, whg, wout, ln_g, ln_b, rw_pad, rb_pad, tri)


def _pair_split_perm():
    p = np.zeros((PAIR_GROUP, PAIR_GROUP), np.float32)
    i = np.arange(LANES)
    p[2 * i, i] = 1.0
    p[2 * i + 1, LANES + i] = 1.0
    return p


def _regroup_kernel(w_ref, perm_ref, o_ref):
    perm = perm_ref[...]
    for c0 in range(0, w_ref.shape[2], PAIR_GROUP):
        cols = slice(c0, c0 + PAIR_GROUP)
        o_ref[0, :, cols] = _dot(w_ref[0, :, cols].astype(BF16), perm).astype(o_ref.dtype)


def _regroup_up_weights(w_up):
    n_exp, d, two_ff = w_up.shape
    cols = 4 * PAIR_GROUP
    perm = jnp.asarray(_pair_split_perm(), BF16)
    return pl.pallas_call(
        _regroup_kernel,
        grid=(n_exp, two_ff // cols),
        in_specs=[pl.BlockSpec((1, d, cols), lambda e, c: (e, 0, c)),
                  pl.BlockSpec(perm.shape, lambda e, c: (0, 0))],
        out_specs=pl.BlockSpec((1, d, cols), lambda e, c: (e, 0, c)),
        out_shape=jax.ShapeDtypeStruct(w_up.shape, BF16),
        compiler_params=_cparams(("arbitrary", "arbitrary")),
        name="regroup",
    )(w_up, perm)


def _expert_kernel(be_ref, nused_ref, tok_ahead_ref, tok_b0_ref, tok_b1_ref, dst_ref,
                   h_ref, wu_ref, bu_ref, wd_ref, bd_ref, yk_ref,
                   xbuf, ybuf, gsem, ssem, *, dump_row0):
    i = pl.program_id(0)
    last = pl.num_programs(0) - 1
    xs = i % X_SLOTS
    ys = i % Y_SLOTS
    d_ff, d = wd_ref.shape[1:]
    pr = d // LANES
    tm = ybuf.shape[0] // (Y_SLOTS * pr)
    span = tm * pr

    def gather(idx_ref, r, s):
        src = h_ref.at[pl.ds(pl.multiple_of(idx_ref[0, 0, r], pr), pr)]
        dst = xbuf.at[pl.ds(pl.multiple_of(s * span + r * pr, pr), pr)]
        return pltpu.make_async_copy(src, dst, gsem.at[s])

    def scatter(r, s):
        src = ybuf.at[pl.ds(pl.multiple_of(s * span + r * pr, pr), pr)]
        dst = yk_ref.at[pl.ds(pl.multiple_of(dst_ref[0, 0, r], pr), pr)]
        return pltpu.make_async_copy(src, dst, ssem.at[s])

    def slot_rows(buf, s):
        return buf.at[pl.ds(pl.multiple_of(s * span, span), span)]

    def wait_gathers(s):
        pltpu.make_async_copy(h_ref.at[pl.ds(0, span)], slot_rows(xbuf, s), gsem.at[s]).wait()

    def wait_scatters(s):
        pltpu.make_async_copy(slot_rows(ybuf, s), yk_ref.at[pl.ds(0, span)], ssem.at[s]).wait()

    def gather_block(idx_ref, s):
        def one(r, _):
            gather(idx_ref, r, s).start(priority=0)
            return 0
        lax.fori_loop(0, tm, one, 0, unroll=ISSUE_UNROLL)

    @pl.when(i == 0)
    def _():
        ybuf[...] = jnp.zeros_like(ybuf)
        for s in range(Y_SLOTS):
            fill = pltpu.make_async_copy(slot_rows(ybuf, s),
                                         yk_ref.at[pl.ds(dump_row0 * pr + s * span, span)],
                                         ssem.at[s])
            fill.start()
            fill.wait()
        gather_block(tok_b0_ref, 0)
        gather_block(tok_b1_ref, 1)

    wait_gathers(xs)

    @pl.when(i >= Y_SLOTS)
    def _():
        wait_scatters(ys)

    @pl.when(i < nused_ref[0])
    def _():
        xb = jnp.concatenate(
            [c.astype(BF16) for c in _load_row_tiles(xbuf, xs * tm, tm, pr)], axis=1)
        acc = jnp.zeros((tm, d), F32) + bd_ref[0]
        for c0 in range(0, d_ff, FF_CHUNK):
            up = _dot(xb, wu_ref[0, :, 2 * c0:2 * (c0 + FF_CHUNK)]) + bu_ref[0, :, 2 * c0:2 * (c0 + FF_CHUNK)]
            acts = []
            for g0 in range(0, 2 * FF_CHUNK, PAIR_GROUP):
                x_glu = jnp.minimum(up[:, g0:g0 + LANES], SWIGLU_LIMIT)
                x_lin = jnp.clip(up[:, g0 + LANES:g0 + PAIR_GROUP], -SWIGLU_LIMIT, SWIGLU_LIMIT)
                acts.append(x_glu * jax.nn.sigmoid(SWIGLU_ALPHA * x_glu) * (x_lin + 1.0))
            act = jnp.concatenate(acts, axis=1)
            acc = acc + _dot(act.astype(BF16), wd_ref[0, c0:c0 + FF_CHUNK, :])
        _store_row_tiles(ybuf, ys * tm, acc)

    @pl.when(i >= nused_ref[0])
    def _():
        ybuf[pl.ds(pl.multiple_of(ys * span, span), span), :] = jnp.zeros((span, LANES), F32)

    def scatter_one(r, _):
        scatter(r, ys).start(priority=1)
        return 0
    lax.fori_loop(0, tm, scatter_one, 0, unroll=ISSUE_UNROLL)
    gather_block(tok_ahead_ref, (i + 2) % X_SLOTS)

    @pl.when(i == last)
    def _():
        wait_scatters(1 - ys)
        wait_scatters(ys)
        wait_gathers((i + 1) % X_SLOTS)
        wait_gathers((i + 2) % X_SLOTS)


def _experts(block_e, n_used, tok_blocks, dst_blocks, h_tiles, wu, bu, wd, bd):
    tm = EXPERT_ROWS
    n_blocks = tok_blocks.shape[0] - 2
    d_ff, d = wd.shape[1:]
    pr = d // LANES
    n = h_tiles.shape[0] // pr
    by_e3 = lambda i, be, nu: (be[i], 0, 0)
    idx_spec = lambda f: pl.BlockSpec((1, 1, tm), f, memory_space=pltpu.SMEM)
    dump_row0 = TOP_K * n
    grid_spec = pltpu.PrefetchScalarGridSpec(
        num_scalar_prefetch=2,
        grid=(n_blocks,),
        in_specs=[idx_spec(lambda i, be, nu: (i + 2, 0, 0)),
                  idx_spec(lambda i, be, nu: (0, 0, 0)),
                  idx_spec(lambda i, be, nu: (1, 0, 0)),
                  idx_spec(lambda i, be, nu: (i, 0, 0)),
                  pl.BlockSpec(memory_space=pl.ANY),
                  pl.BlockSpec((1, d, 2 * d_ff), by_e3),
                  pl.BlockSpec((1, 1, 2 * d_ff), by_e3),
                  pl.BlockSpec((1, d_ff, d), by_e3),
                  pl.BlockSpec((1, 1, d), by_e3)],
        out_specs=pl.BlockSpec(memory_space=pl.ANY),
        scratch_shapes=[pltpu.VMEM((X_SLOTS * tm * pr, LANES), F32),
                        pltpu.VMEM((Y_SLOTS * tm * pr, LANES), F32),
                        pltpu.SemaphoreType.DMA((X_SLOTS,)), pltpu.SemaphoreType.DMA((Y_SLOTS,))],
    )
    return pl.pallas_call(
        functools.partial(_expert_kernel, dump_row0=dump_row0),
        grid_spec=grid_spec,
        out_shape=jax.ShapeDtypeStruct(((dump_row0 + 2 * tm) * pr, LANES), F32),
        compiler_params=_cparams(("arbitrary",)),
        name="experts",
    )(block_e, n_used, tok_blocks * pr, tok_blocks * pr, tok_blocks * pr, dst_blocks * pr,
      h_tiles, wu, bu, wd, bd)


def _combine_kernel(y0_ref, y1_ref, y2_ref, y3_ref, h_ref, gate_ref, g2_ref, b2_ref, o_ref):
    rows, d = o_ref.shape
    pr = d // LANES
    gates = gate_ref[...]
    pre = [DEEPNORM_ALPHA * hc for hc in _load_row_tiles(h_ref, 0, rows, pr)]
    for kk, y_ref in enumerate((y0_ref, y1_ref, y2_ref, y3_ref)):
        gate = gates[:, kk:kk + 1]
        pre = [p + yc * gate for p, yc in zip(pre, _load_row_tiles(y_ref, 0, rows, pr))]
    o_ref[...] = _layer_norm(jnp.concatenate(pre, axis=1), g2_ref[...], b2_ref[...])


def _combine(yk_tiles, h_tiles, gates, ln_g, ln_b):
    d = ln_g.shape[1]
    pr = d // LANES
    n = h_tiles.shape[0] // pr
    rows = COMBINE_ROWS
    per_choice = n // rows
    choice = lambda kk: pl.BlockSpec((rows * pr, LANES), lambda i: (kk * per_choice + i, 0))
    return pl.pallas_call(
        _combine_kernel,
        grid=(per_choice,),
        in_specs=[choice(0), choice(1), choice(2), choice(3),
                  pl.BlockSpec((rows * pr, LANES), lambda i: (i, 0)),
                  pl.BlockSpec((rows, LANES), lambda i: (i, 0)),
                  pl.BlockSpec(ln_g.shape, lambda i: (0, 0)),
                  pl.BlockSpec(ln_b.shape, lambda i: (0, 0))],
        out_specs=pl.BlockSpec((rows, d), lambda i: (i, 0)),
        out_shape=jax.ShapeDtypeStruct((n, d), F32),
        compiler_params=_cparams(("arbitrary",)),
        name="combine",
    )(yk_tiles, yk_tiles, yk_tiles, yk_tiles, h_tiles, gates, ln_g, ln_b)


def _slot_tables(idx_slab, pos_slab, cnt, n, n_exp):
    tm = EXPERT_ROWS
    n_asg = n * TOP_K
    counts = cnt[0, :n_exp].astype(I32)
    padded = (counts + tm - 1) // tm * tm
    pad_end = jnp.cumsum(padded)
    pad_start = pad_end - padded
    start = jnp.cumsum(counts) - counts
    dest = (pad_start[idx_slab[:, :TOP_K]] + pos_slab[:, :TOP_K]).reshape(-1).astype(I32)
    n_blocks = -(-(n_asg + n_exp * (tm - 1)) // tm)
    block_row0 = jnp.arange(n_blocks, dtype=I32) * tm
    block_e = jnp.minimum(
        jnp.sum((pad_end[None, :] <= block_row0[:, None]).astype(I32), axis=1), n_exp - 1)
    n_used = (pad_end[-1:] // tm).astype(I32)

    _, order = lax.sort_key_val(dest, jnp.arange(n_asg, dtype=I32))
    slots = jnp.arange(n_blocks * tm, dtype=I32)
    e_slot = jnp.repeat(block_e, tm)
    offset = slots - pad_start[e_slot]
    valid = offset < counts[e_slot]
    asg = order[jnp.clip(start[e_slot] + offset, 0, n_asg - 1)]
    tok = jnp.where(valid, asg // TOP_K, 0)
    dst = jnp.where(valid, (asg % TOP_K) * n + tok, n_asg + slots % (2 * tm))
    tok_blocks = jnp.concatenate([tok, jnp.zeros((2 * tm,), I32)]).reshape(n_blocks + 2, 1, tm)
    return block_e, n_used, tok_blocks, dst.reshape(n_blocks, 1, tm)


def kernel(x, w_in, hgrn_lb_logits, hgrn_norm_g, w_branch_sb, w_branch_hgrn, w_out, ln1_g, ln1_b,
           router_w, router_b, expert_w_up, expert_b_up, expert_w_down, expert_b_down, ln2_g, ln2_b):
    b, t, d = x.shape
    assert w_in.shape[0] == DEPTH and hgrn_lb_logits.shape[0] == DEPTH + 1
    n = b * t
    sbw = w_branch_sb.shape[1]
    hgw = w_branch_hgrn.shape[1]
    widths = (sbw, sbw, sbw, hgw, hgw, hgw, hgw, d, d)
    assert sum(widths) == w_in.shape[2]
    n_exp, _, two_ff = expert_w_up.shape[1:]
    assert n_exp == N_EXPERTS

    x2 = x.reshape(n, d)
    sq, sk, sv, hq, hf, hi, hg, g_sb, g_hg = _inproj(x2, w_in[0].astype(BF16), widths)

    seq = lambda a: a.reshape(b, t, a.shape[-1])
    o_sb = _attention(seq(sq), seq(sk), seq(sv)).reshape(n, sbw)
    o_hg = _hgrn(seq(hq), seq(hf), seq(hi), seq(hg), hgrn_lb_logits, hgrn_norm_g).reshape(n, hgw)

    rw_pad = jnp.zeros((d, LANES), F32).at[:, :n_exp].set(router_w[0])
    rb_pad = jnp.full((1, LANES), NEG_BIG, F32).at[0, :n_exp].set(router_b[0])
    h, idx_slab, gate_slab, pos_slab, cnt = _mix(
        x2, o_sb, o_hg, g_sb, g_hg,
        w_branch_sb[0].astype(BF16), w_branch_hgrn[0].astype(BF16), w_out[0].astype(BF16),
        ln1_g, ln1_b, rw_pad, rb_pad)

    block_e, n_used, tok_blocks, dst_blocks = _slot_tables(idx_slab, pos_slab, cnt, n, n_exp)

    wu = _regroup_up_weights(expert_w_up[0])
    bu = expert_b_up[0].reshape(n_exp, two_ff // PAIR_GROUP, LANES, 2)
    bu = bu.transpose(0, 1, 3, 2).reshape(n_exp, 1, two_ff)
    wd = expert_w_down[0].astype(BF16)
    bd = expert_b_down[0][:, None, :]
    yk = _experts(block_e, n_used, tok_blocks, dst_blocks, h, wu, bu, wd, bd)

    out = _combine(yk, h, gate_slab, ln2_g, ln2_b)
    return out.reshape(b, t, d)
```

```python
import functools
import math

import jax
import jax.numpy as jnp
import numpy as np
from jax import lax
from jax.experimental import pallas as pl
from jax.experimental.pallas import tpu as pltpu

F32 = jnp.float32
BF16 = jnp.bfloat16
I32 = jnp.int32

LANES = 128
SUBLANES = 8
PAIR_GROUP = 2 * LANES
VMEM_LIMIT_BYTES = 56 * 1024 * 1024

SB_HEAD_DIM = 64
HGRN_HEAD_DIM = 128
N_EXPERTS = 32
TOP_K = 4
SWIGLU_ALPHA = 1.702
SWIGLU_LIMIT = 7.0
DEPTH = 1
DEEPNORM_ALPHA = (2 * DEPTH) ** 0.25
LN_EPS = 1e-5
RMS_EPS = 1e-6

INPROJ_ROWS = 512
ATTN_BLOCK = 256
HGRN_CHUNK = 64
HGRN_STEP_ROWS = 256
HGRN_DIRECT = 8
MIX_ROWS = 256
EXPERT_ROWS = 256
X_SLOTS = 3
Y_SLOTS = 2
ISSUE_UNROLL = 8
FF_CHUNK = 512
COMBINE_ROWS = 256
NEG_BIG = -1e30
LOG2E = math.log2(math.e)


def _cparams(semantics):
    return pltpu.CompilerParams(dimension_semantics=semantics, vmem_limit_bytes=VMEM_LIMIT_BYTES)


def _split_bf16(a):
    hi = a.astype(BF16)
    lo = (a - hi.astype(F32)).astype(BF16)
    return hi, lo


def _split_trunc(a):
    bits = lax.bitcast_convert_type(a, jnp.uint32) & jnp.uint32(0xFFFF0000)
    hi = lax.bitcast_convert_type(bits, F32)
    return hi.astype(BF16), (a - hi).astype(BF16)


def _dot_nt(a, b):
    return lax.dot_general(a, b, (((1,), (1,)), ((), ())), preferred_element_type=F32)


def _dot_tn(a, b):
    return lax.dot_general(a, b, (((0,), (0,)), ((), ())), preferred_element_type=F32)


def _dot(a, b):
    return jnp.dot(a, b, preferred_element_type=F32)


def _inproj_kernel(x_ref, w_ref, *out_refs, widths):
    xb = x_ref[...].astype(BF16)
    off = 0
    for ref, width in zip(out_refs, widths):
        ref[...] = _dot(xb, w_ref[:, off:off + width]).astype(ref.dtype)
        off += width


def _inproj(x2, w_bf, widths):
    n, d = x2.shape
    tm = INPROJ_ROWS
    out_shape = [jax.ShapeDtypeStruct((n, w), BF16) for w in widths]
    out_specs = [pl.BlockSpec((tm, w), lambda i: (i, 0)) for w in widths]
    return pl.pallas_call(
        functools.partial(_inproj_kernel, widths=widths),
        grid=(n // tm,),
        in_specs=[pl.BlockSpec((tm, d), lambda i: (i, 0)),
                  pl.BlockSpec(w_bf.shape, lambda i: (0, 0))],
        out_specs=out_specs,
        out_shape=out_shape,
        compiler_params=_cparams(("arbitrary",)),
        name="inproj",
    )(x2, w_bf)


def _attn_kernel(q_ref, k_ref, v_ref, tri_ref, o_ref, *, blk, scale):
    qi = pl.program_id(2)
    q = q_ref[0].astype(F32) * scale
    lane = lax.broadcasted_iota(I32, q.shape, 1)
    first = lane < SB_HEAD_DIM
    q_heads = (jnp.where(first, q, 0.0).astype(BF16), jnp.where(first, 0.0, q).astype(BF16))
    tri2 = tri_ref[...]
    row = lax.broadcasted_iota(I32, (blk, blk), 0)
    col = lax.broadcasted_iota(I32, (blk, blk), 1)
    strictly_before = col < row
    heads = range(len(q_heads))

    def scores(kb):
        start = pl.multiple_of(kb * blk, blk)
        ks = k_ref[0, pl.ds(start, blk), :]
        return tuple(_dot_nt(qh, ks) for qh in q_heads)

    def finish(kb, zs, carry, diagonal):
        start = pl.multiple_of(kb * blk, blk)
        vs = v_ref[0, pl.ds(start, blk), :]
        ts = [jnp.log(1.0 + jnp.exp2(jnp.abs(z) * -LOG2E)) for z in zs]
        log_beta = [jnp.minimum(zs[h], 0.0) - ts[h] for h in heads]
        log_keep = [log_beta[h] - zs[h] for h in heads]
        if diagonal:
            log_keep = [jnp.where(strictly_before, lk, 0.0) for lk in log_keep]
        parts = [jnp.concatenate(_split_trunc(lk), axis=1) for lk in log_keep]
        log_after = [_dot(parts[h], tri2) + carry[h][1] for h in heads]
        ws = [jnp.exp(log_beta[h] + log_after[h]) for h in heads]
        if diagonal:
            ws = [jnp.where(strictly_before, w, 0.0) for w in ws]
        return tuple((carry[h][0] + _dot(ws[h].astype(BF16), vs),
                      carry[h][1] + jnp.sum(log_keep[h], axis=1, keepdims=True)) for h in heads)

    zero = (jnp.zeros((blk, LANES), F32), jnp.zeros((blk, 1), F32))
    carry = finish(qi, scores(qi), (zero, zero), True)

    def body(j, state):
        zs, carry = state
        kb = qi - 1 - j
        zs_next = scores(jnp.maximum(kb - 1, 0))
        return zs_next, finish(kb, zs, carry, False)

    _, carry = lax.fori_loop(0, qi, body, (scores(jnp.maximum(qi - 1, 0)), carry))
    o_ref[0] = jnp.where(first, carry[0][0], carry[1][0]).astype(o_ref.dtype)


def _attention(q, k, v):
    b, t, w = q.shape
    blk = ATTN_BLOCK
    pairs = w // LANES
    idx = np.arange(blk)
    tri = idx[:, None] > idx[None, :]
    tri = jnp.asarray(np.concatenate([tri, tri], axis=0), BF16)
    return pl.pallas_call(
        functools.partial(_attn_kernel, blk=blk, scale=SB_HEAD_DIM ** -0.5),
        grid=(b, pairs, t // blk),
        in_specs=[pl.BlockSpec((1, blk, LANES), lambda bi, hp, qi: (bi, qi, hp)),
                  pl.BlockSpec((1, t, LANES), lambda bi, hp, qi: (bi, 0, hp)),
                  pl.BlockSpec((1, t, LANES), lambda bi, hp, qi: (bi, 0, hp)),
                  pl.BlockSpec((2 * blk, blk), lambda bi, hp, qi: (0, 0))],
        out_specs=pl.BlockSpec((1, blk, LANES), lambda bi, hp, qi: (bi, qi, hp)),
        out_shape=jax.ShapeDtypeStruct((b, t, w), BF16),
        compiler_params=_cparams(("arbitrary", "arbitrary", "arbitrary")),
        name="attn",
    )(q, k, v, tri)


def _hgrn_level_sizes():
    sizes = []
    s = HGRN_CHUNK
    while s > HGRN_DIRECT:
        sizes.append(s)
        s //= 2
    return tuple(sizes)


def _hgrn_kernel(hq_ref, hf_ref, hi_ref, hg_ref, lbl_ref, ng_ref, tri_ref, mask_ref, o_ref, st_ref,
                 *, chunks):
    c = HGRN_CHUNK
    d = HGRN_DIRECT
    groups = c // d

    @pl.when(pl.program_id(2) == 0)
    def _():
        st_ref[...] = jnp.zeros_like(st_ref)

    logits = lbl_ref[...].astype(F32)
    ex = jnp.exp(logits - jnp.max(logits, axis=0, keepdims=True))
    lb = ex[0:1, :] / jnp.sum(ex, axis=0, keepdims=True)
    norm_g = ng_ref[...]
    tri = tri_ref[...]
    sub = lax.broadcasted_iota(I32, (groups, d, LANES), 1)

    for ci in range(chunks):
        rows = slice(ci * c, (ci + 1) * c)
        f = lb + (1.0 - lb) * jax.nn.sigmoid(hf_ref[0, rows, :].astype(F32))
        g = jnp.log(f)
        kk = 1.0 - f
        qq = jax.nn.silu(hq_ref[0, rows, :].astype(F32))
        vv = hi_ref[0, rows, :].astype(F32)
        vb = vv.astype(BF16)

        g_hi, g_lo = _split_bf16(g)
        bcum = _dot(tri, g_hi) + _dot(tri, g_lo)
        b_last = bcum[c - 1:c, :]

        st = st_ref[...]
        o = _dot_nt((qq * jnp.exp(bcum)).astype(BF16), st.astype(BF16))

        scores = jnp.zeros((c, c), F32)
        for li, size in enumerate(_hgrn_level_sizes()):
            half = size // 2
            ref_rows = jnp.concatenate(
                [jnp.broadcast_to(bcum[bi * size + half - 1:bi * size + half, :], (size, LANES))
                 for bi in range(c // size)], axis=0)
            q_dec = qq * jnp.exp(jnp.minimum(bcum - ref_rows, 0.0))
            k_dec = kk * jnp.exp(jnp.minimum(ref_rows - bcum, 0.0))
            sc = _dot_nt(q_dec.astype(BF16), k_dec.astype(BF16))
            scores = scores + jnp.where(mask_ref[li] > 0.0, sc, 0.0)
        o = o + _dot(scores.astype(BF16), vb)

        b3 = bcum.reshape(groups, d, LANES)
        q3 = qq.reshape(groups, d, LANES)
        k3 = kk.reshape(groups, d, LANES)
        v3 = vv.reshape(groups, d, LANES)
        od = jnp.zeros((groups, d, LANES), F32)
        for j in range(d):
            e = jnp.exp(jnp.minimum(b3 - b3[:, j:j + 1, :], 0.0))
            colj = jnp.sum(q3 * e * k3[:, j:j + 1, :], axis=-1, keepdims=True)
            colj = jnp.where(sub[:, :, 0:1] >= j, colj, 0.0)
            od = od + colj * v3[:, j:j + 1, :]
        o = o + od.reshape(c, LANES)

        k_dec = kk * jnp.exp(b_last - bcum)
        st_ref[...] = st * jnp.exp(b_last) + _dot_tn(vb, k_dec.astype(BF16))

        o = o * lax.rsqrt(jnp.mean(jnp.square(o), axis=-1, keepdims=True) + RMS_EPS)
        o = o * norm_g * jax.nn.sigmoid(hg_ref[0, rows, :].astype(F32))
        o_ref[0, rows, :] = o.astype(o_ref.dtype)


def _hgrn(hq, hf, hi, hg, lb_logits, norm_g):
    b, t, w = hq.shape
    heads = w // HGRN_HEAD_DIM
    c = HGRN_CHUNK
    rows = min(HGRN_STEP_ROWS, t)
    idx = np.arange(c)
    tri = jnp.asarray(idx[:, None] >= idx[None, :], BF16)
    masks = []
    for size in _hgrn_level_sizes():
        half = size // 2
        same = (idx[:, None] // size) == (idx[None, :] // size)
        masks.append(same & ((idx[:, None] % size) >= half) & ((idx[None, :] % size) < half))
    masks = jnp.asarray(np.stack(masks), F32)
    seq_spec = pl.BlockSpec((1, rows, LANES), lambda bi, h, ti: (bi, ti, h))
    return pl.pallas_call(
        functools.partial(_hgrn_kernel, chunks=rows // c),
        grid=(b, heads, t // rows),
        in_specs=[seq_spec, seq_spec, seq_spec, seq_spec,
                  pl.BlockSpec((lb_logits.shape[0], LANES), lambda bi, h, ti: (0, h)),
                  pl.BlockSpec((1, LANES), lambda bi, h, ti: (0, h)),
                  pl.BlockSpec((c, c), lambda bi, h, ti: (0, 0)),
                  pl.BlockSpec(masks.shape, lambda bi, h, ti: (0, 0, 0))],
        out_specs=seq_spec,
        out_shape=jax.ShapeDtypeStruct((b, t, w), BF16),
        scratch_shapes=[pltpu.VMEM((HGRN_HEAD_DIM, HGRN_HEAD_DIM), F32)],
        compiler_params=_cparams(("arbitrary", "arbitrary", "arbitrary")),
        name="hgrn",
    )(hq, hf, hi, hg, lb_logits, norm_g, tri, masks)


def _store_row_tiles(ref, row0, value):
    rows, width = value.shape
    per_row = width // LANES
    for c in range(per_row):
        ref[pl.ds(row0 * per_row + c, rows, stride=per_row), :] = value[:, c * LANES:(c + 1) * LANES]


def _load_row_tiles(ref, row0, rows, per_row):
    return [ref[pl.ds(row0 * per_row + c, rows, stride=per_row), :] for c in range(per_row)]


def _layer_norm(v, g, b):
    mu = jnp.mean(v, axis=-1, keepdims=True)
    cen = v - mu
    var = jnp.mean(jnp.square(cen), axis=-1, keepdims=True)
    return cen * lax.rsqrt(var + LN_EPS) * g + b


def _mix_kernel(x_ref, osb_ref, ohg_ref, gsb_ref, ghg_ref, wsb_ref, whg_ref, wout_ref,
                g1_ref, b1_ref, rw_ref, rb_ref, tri_ref,
                h_ref, idx_ref, gate_ref, pos_ref, cnt_ref, run_ref):
    @pl.when(pl.program_id(0) == 0)
    def _():
        run_ref[...] = jnp.zeros_like(run_ref)

    merged = (jax.nn.sigmoid(gsb_ref[...].astype(F32)) * _dot(osb_ref[...], wsb_ref[...])
              + jax.nn.sigmoid(ghg_ref[...].astype(F32)) * _dot(ohg_ref[...], whg_ref[...]))
    mix = _dot(merged.astype(BF16), wout_ref[...])
    h = _layer_norm(DEEPNORM_ALPHA * x_ref[...] + mix, g1_ref[...], b1_ref[...])
    _store_row_tiles(h_ref, 0, h)

    h_hi, h_lo = _split_bf16(h)
    h_lo2 = (h - h_hi.astype(F32) - h_lo.astype(F32)).astype(BF16)
    rw = rw_ref[...]
    rw_hi, rw_lo = _split_bf16(rw)
    rw_lo2 = (rw - rw_hi.astype(F32) - rw_lo.astype(F32)).astype(BF16)
    logits = (_dot(h_hi, rw_hi) + (_dot(h_hi, rw_lo) + _dot(h_lo, rw_hi))
              + (_dot(h_lo, rw_lo) + _dot(h_hi, rw_lo2) + _dot(h_lo2, rw_hi))) + rb_ref[...]

    tm = logits.shape[0]
    lane = lax.broadcasted_iota(I32, (tm, LANES), 1)
    remaining = logits
    vals, idxs, hots = [], [], []
    for _ in range(TOP_K):
        m = jnp.max(remaining, axis=-1, keepdims=True)
        sel = jnp.min(jnp.where(remaining == m, lane, LANES), axis=-1, keepdims=True)
        hot = lane == sel
        remaining = jnp.where(hot, NEG_BIG, remaining)
        vals.append(m)
        idxs.append(sel)
        hots.append(hot)
    exps = [jnp.exp(v - vals[0]) for v in vals]
    denom = exps[0] + exps[1] + exps[2] + exps[3]

    chosen = jnp.where(hots[0] | hots[1] | hots[2] | hots[3], 1.0, 0.0)
    before = _dot(tri_ref[...], chosen.astype(BF16)) + run_ref[0:1, :]
    idx_slab = jnp.zeros((tm, LANES), I32)
    gate_slab = jnp.zeros((tm, LANES), F32)
    pos_slab = jnp.zeros((tm, LANES), I32)
    for kk in range(TOP_K):
        rank = jnp.sum(jnp.where(hots[kk], before, 0.0), axis=-1, keepdims=True)
        here = lane == kk
        idx_slab = jnp.where(here, idxs[kk], idx_slab)
        gate_slab = jnp.where(here, exps[kk] / denom, gate_slab)
        pos_slab = jnp.where(here, rank.astype(I32), pos_slab)
    idx_ref[...] = idx_slab
    gate_ref[...] = gate_slab
    pos_ref[...] = pos_slab
    run_ref[0:1, :] = run_ref[0:1, :] + jnp.sum(chosen, axis=0, keepdims=True)
    cnt_ref[...] = run_ref[...]


def _mix(x2, o_sb, o_hg, g_sb, g_hg, wsb, whg, wout, ln_g, ln_b, rw_pad, rb_pad):
    n, d = x2.shape
    tm = MIX_ROWS
    idx = np.arange(tm)
    tri = jnp.asarray(idx[:, None] > idx[None, :], BF16)
    row = lambda w: pl.BlockSpec((tm, w), lambda i: (i, 0))
    full = lambda a: pl.BlockSpec(a.shape, lambda i: (0,) * a.ndim)
    return pl.pallas_call(
        _mix_kernel,
        grid=(n // tm,),
        in_specs=[row(d), row(o_sb.shape[1]), row(o_hg.shape[1]), row(d), row(d),
                  full(wsb), full(whg), full(wout), full(ln_g), full(ln_b),
                  full(rw_pad), full(rb_pad), full(tri)],
        out_specs=[pl.BlockSpec((tm * d // LANES, LANES), lambda i: (i, 0)),
                   row(LANES), row(LANES), row(LANES),
                   pl.BlockSpec((SUBLANES, LANES), lambda i: (0, 0))],
        out_shape=[jax.ShapeDtypeStruct((n * d // LANES, LANES), F32),
                   jax.ShapeDtypeStruct((n, LANES), I32),
                   jax.ShapeDtypeStruct((n, LANES), F32),
                   jax.ShapeDtypeStruct((n, LANES), I32),
                   jax.ShapeDtypeStruct((SUBLANES, LANES), F32)],
        scratch_shapes=[pltpu.VMEM((SUBLANES, LANES), F32)],
        compiler_params=_cparams(("arbitrary",)),
        name="mix",
    )(x2, o_sb, o_hg, g_sb, g_hg, wsb, whg, wout, ln_g, ln_b, rw_pad, rb_pad, tri)


def _pair_split_perm():
    p = np.zeros((PAIR_GROUP, PAIR_GROUP), np.float32)
    i = np.arange(LANES)
    p[2 * i, i] = 1.0
    p[2 * i + 1, LANES + i] = 1.0
    return p


def _regroup_kernel(w_ref, perm_ref, o_ref):
    perm = perm_ref[...]
    for c0 in range(0, w_ref.shape[2], PAIR_GROUP):
        cols = slice(c0, c0 + PAIR_GROUP)
        o_ref[0, :, cols] = _dot(w_ref[0, :, cols].astype(BF16), perm).astype(o_ref.dtype)


def _regroup_up_weights(w_up):
    n_exp, d, two_ff = w_up.shape
    cols = 4 * PAIR_GROUP
    perm = jnp.asarray(_pair_split_perm(), BF16)
    return pl.pallas_call(
        _regroup_kernel,
        grid=(n_exp, two_ff // cols),
        in_specs=[pl.BlockSpec((1, d, cols), lambda e, c: (e, 0, c)),
                  pl.BlockSpec(perm.shape, lambda e, c: (0, 0))],
        out_specs=pl.BlockSpec((1, d, cols), lambda e, c: (e, 0, c)),
        out_shape=jax.ShapeDtypeStruct(w_up.shape, BF16),
        compiler_params=_cparams(("arbitrary", "arbitrary")),
        name="regroup",
    )(w_up, perm)


def _expert_kernel(be_ref, nused_ref, tok_ahead_ref, tok_b0_ref, tok_b1_ref, dst_ref,
                   h_ref, wu_ref, bu_ref, wd_ref, bd_ref, yk_ref,
                   xbuf, ybuf, gsem, ssem, *, dump_row0):
    i = pl.program_id(0)
    last = pl.num_programs(0) - 1
    xs = i % X_SLOTS
    ys = i % Y_SLOTS
    d_ff, d = wd_ref.shape[1:]
    pr = d // LANES
    tm = ybuf.shape[0] // (Y_SLOTS * pr)
    span = tm * pr

    def gather(idx_ref, r, s):
        src = h_ref.at[pl.ds(pl.multiple_of(idx_ref[0, 0, r], pr), pr)]
        dst = xbuf.at[pl.ds(pl.multiple_of(s * span + r * pr, pr), pr)]
        return pltpu.make_async_copy(src, dst, gsem.at[s])

    def scatter(r, s):
        src = ybuf.at[pl.ds(pl.multiple_of(s * span + r * pr, pr), pr)]
        dst = yk_ref.at[pl.ds(pl.multiple_of(dst_ref[0, 0, r], pr), pr)]
        return pltpu.make_async_copy(src, dst, ssem.at[s])

    def slot_rows(buf, s):
        return buf.at[pl.ds(pl.multiple_of(s * span, span), span)]

    def wait_gathers(s):
        pltpu.make_async_copy(h_ref.at[pl.ds(0, span)], slot_rows(xbuf, s), gsem.at[s]).wait()

    def wait_scatters(s):
        pltpu.make_async_copy(slot_rows(ybuf, s), yk_ref.at[pl.ds(0, span)], ssem.at[s]).wait()

    def gather_block(idx_ref, s):
        def one(r, _):
            gather(idx_ref, r, s).start(priority=0)
            return 0
        lax.fori_loop(0, tm, one, 0, unroll=ISSUE_UNROLL)

    @pl.when(i == 0)
    def _():
        ybuf[...] = jnp.zeros_like(ybuf)
        for s in range(Y_SLOTS):
            fill = pltpu.make_async_copy(slot_rows(ybuf, s),
                                         yk_ref.at[pl.ds(dump_row0 * pr + s * span, span)],
                                         ssem.at[s])
            fill.start()
            fill.wait()
        gather_block(tok_b0_ref, 0)
        gather_block(tok_b1_ref, 1)

    wait_gathers(xs)

    @pl.when(i >= Y_SLOTS)
    def _():
        wait_scatters(ys)

    @pl.when(i < nused_ref[0])
    def _():
        xb = jnp.concatenate(
            [c.astype(BF16) for c in _load_row_tiles(xbuf, xs * tm, tm, pr)], axis=1)
        acc = jnp.zeros((tm, d), F32) + bd_ref[0]
        for c0 in range(0, d_ff, FF_CHUNK):
            up = _dot(xb, wu_ref[0, :, 2 * c0:2 * (c0 + FF_CHUNK)]) + bu_ref[0, :, 2 * c0:2 * (c0 + FF_CHUNK)]
            acts = []
            for g0 in range(0, 2 * FF_CHUNK, PAIR_GROUP):
                x_glu = jnp.minimum(up[:, g0:g0 + LANES], SWIGLU_LIMIT)
                x_lin = jnp.clip(up[:, g0 + LANES:g0 + PAIR_GROUP], -SWIGLU_LIMIT, SWIGLU_LIMIT)
                acts.append(x_glu * jax.nn.sigmoid(SWIGLU_ALPHA * x_glu) * (x_lin + 1.0))
            act = jnp.concatenate(acts, axis=1)
            acc = acc + _dot(act.astype(BF16), wd_ref[0, c0:c0 + FF_CHUNK, :])
        _store_row_tiles(ybuf, ys * tm, acc)

    @pl.when(i >= nused_ref[0])
    def _():
        ybuf[pl.ds(pl.multiple_of(ys * span, span), span), :] = jnp.zeros((span, LANES), F32)

    ahead = (i + 2) % X_SLOTS

    def move_rows(r, _):
        scatter(r, ys).start(priority=1)
        gather(tok_ahead_ref, r, ahead).start(priority=0)
        return 0
    lax.fori_loop(0, tm, move_rows, 0, unroll=ISSUE_UNROLL)

    @pl.when(i == last)
    def _():
        wait_scatters(1 - ys)
        wait_scatters(ys)
        wait_gathers((i + 1) % X_SLOTS)
        wait_gathers((i + 2) % X_SLOTS)


def _experts(block_e, n_used, tok_blocks, dst_blocks, h_tiles, wu, bu, wd, bd):
    tm = EXPERT_ROWS
    n_blocks = tok_blocks.shape[0] - 2
    d_ff, d = wd.shape[1:]
    pr = d // LANES
    n = h_tiles.shape[0] // pr
    by_e3 = lambda i, be, nu: (be[i], 0, 0)
    idx_spec = lambda f: pl.BlockSpec((1, 1, tm), f, memory_space=pltpu.SMEM)
    dump_row0 = TOP_K * n
    grid_spec = pltpu.PrefetchScalarGridSpec(
        num_scalar_prefetch=2,
        grid=(n_blocks,),
        in_specs=[idx_spec(lambda i, be, nu: (i + 2, 0, 0)),
                  idx_spec(lambda i, be, nu: (0, 0, 0)),
                  idx_spec(lambda i, be, nu: (1, 0, 0)),
                  idx_spec(lambda i, be, nu: (i, 0, 0)),
                  pl.BlockSpec(memory_space=pl.ANY),
                  pl.BlockSpec((1, d, 2 * d_ff), by_e3),
                  pl.BlockSpec((1, 1, 2 * d_ff), by_e3),
                  pl.BlockSpec((1, d_ff, d), by_e3),
                  pl.BlockSpec((1, 1, d), by_e3)],
        out_specs=pl.BlockSpec(memory_space=pl.ANY),
        scratch_shapes=[pltpu.VMEM((X_SLOTS * tm * pr, LANES), F32),
                        pltpu.VMEM((Y_SLOTS * tm * pr, LANES), F32),
                        pltpu.SemaphoreType.DMA((X_SLOTS,)), pltpu.SemaphoreType.DMA((Y_SLOTS,))],
    )
    return pl.pallas_call(
        functools.partial(_expert_kernel, dump_row0=dump_row0),
        grid_spec=grid_spec,
        out_shape=jax.ShapeDtypeStruct(((dump_row0 + 2 * tm) * pr, LANES), F32),
        compiler_params=_cparams(("arbitrary",)),
        name="experts",
    )(block_e, n_used, tok_blocks * pr, tok_blocks * pr, tok_blocks * pr, dst_blocks * pr,
      h_tiles, wu, bu, wd, bd)


def _combine_kernel(y0_ref, y1_ref, y2_ref, y3_ref, h_ref, gate_ref, g2_ref, b2_ref, o_ref):
    rows, d = o_ref.shape
    pr = d // LANES
    gates = gate_ref[...]
    pre = [DEEPNORM_ALPHA * hc for hc in _load_row_tiles(h_ref, 0, rows, pr)]
    for kk, y_ref in enumerate((y0_ref, y1_ref, y2_ref, y3_ref)):
        gate = gates[:, kk:kk + 1]
        pre = [p + yc * gate for p, yc in zip(pre, _load_row_tiles(y_ref, 0, rows, pr))]
    o_ref[...] = _layer_norm(jnp.concatenate(pre, axis=1), g2_ref[...], b2_ref[...])


def _combine(yk_tiles, h_tiles, gates, ln_g, ln_b):
    d = ln_g.shape[1]
    pr = d // LANES
    n = h_tiles.shape[0] // pr
    rows = COMBINE_ROWS
    per_choice = n // rows
    choice = lambda kk: pl.BlockSpec((rows * pr, LANES), lambda i: (kk * per_choice + i, 0))
    return pl.pallas_call(
        _combine_kernel,
        grid=(per_choice,),
        in_specs=[choice(0), choice(1), choice(2), choice(3),
                  pl.BlockSpec((rows * pr, LANES), lambda i: (i, 0)),
                  pl.BlockSpec((rows, LANES), lambda i: (i, 0)),
                  pl.BlockSpec(ln_g.shape, lambda i: (0, 0)),
                  pl.BlockSpec(ln_b.shape, lambda i: (0, 0))],
        out_specs=pl.BlockSpec((rows, d), lambda i: (i, 0)),
        out_shape=jax.ShapeDtypeStruct((n, d), F32),
        compiler_params=_cparams(("arbitrary",)),
        name="combine",
    )(yk_tiles, yk_tiles, yk_tiles, yk_tiles, h_tiles, gates, ln_g, ln_b)


def _slot_tables(idx_slab, pos_slab, cnt, n, n_exp):
    tm = EXPERT_ROWS
    n_asg = n * TOP_K
    counts = cnt[0, :n_exp].astype(I32)
    padded = (counts + tm - 1) // tm * tm
    pad_end = jnp.cumsum(padded)
    pad_start = pad_end - padded
    start = jnp.cumsum(counts) - counts
    dest = (pad_start[idx_slab[:, :TOP_K]] + pos_slab[:, :TOP_K]).reshape(-1).astype(I32)
    n_blocks = -(-(n_asg + n_exp * (tm - 1)) // tm)
    block_row0 = jnp.arange(n_blocks, dtype=I32) * tm
    block_e = jnp.minimum(
        jnp.sum((pad_end[None, :] <= block_row0[:, None]).astype(I32), axis=1), n_exp - 1)
    n_used = (pad_end[-1:] // tm).astype(I32)

    _, order = lax.sort_key_val(dest, jnp.arange(n_asg, dtype=I32))
    slots = jnp.arange(n_blocks * tm, dtype=I32)
    e_slot = jnp.repeat(block_e, tm)
    offset = slots - pad_start[e_slot]
    valid = offset < counts[e_slot]
    asg = order[jnp.clip(start[e_slot] + offset, 0, n_asg - 1)]
    tok = jnp.where(valid, asg // TOP_K, 0)
    dst = jnp.where(valid, (asg % TOP_K) * n + tok, n_asg + slots % (2 * tm))
    tok_blocks = jnp.concatenate([tok, jnp.zeros((2 * tm,), I32)]).reshape(n_blocks + 2, 1, tm)
    return block_e, n_used, tok_blocks, dst.reshape(n_blocks, 1, tm)


def kernel(x, w_in, hgrn_lb_logits, hgrn_norm_g, w_branch_sb, w_branch_hgrn, w_out, ln1_g, ln1_b,
           router_w, router_b, expert_w_up, expert_b_up, expert_w_down, expert_b_down, ln2_g, ln2_b):
    b, t, d = x.shape
    assert w_in.shape[0] == DEPTH and hgrn_lb_logits.shape[0] == DEPTH + 1
    n = b * t
    sbw = w_branch_sb.shape[1]
    hgw = w_branch_hgrn.shape[1]
    widths = (sbw, sbw, sbw, hgw, hgw, hgw, hgw, d, d)
    assert sum(widths) == w_in.shape[2]
    n_exp, _, two_ff = expert_w_up.shape[1:]
    assert n_exp == N_EXPERTS

    x2 = x.reshape(n, d)
    sq, sk, sv, hq, hf, hi, hg, g_sb, g_hg = _inproj(x2, w_in[0].astype(BF16), widths)

    seq = lambda a: a.reshape(b, t, a.shape[-1])
    o_sb = _attention(seq(sq), seq(sk), seq(sv)).reshape(n, sbw)
    o_hg = _hgrn(seq(hq), seq(hf), seq(hi), seq(hg), hgrn_lb_logits, hgrn_norm_g).reshape(n, hgw)

    rw_pad = jnp.zeros((d, LANES), F32).at[:, :n_exp].set(router_w[0])
    rb_pad = jnp.full((1, LANES), NEG_BIG, F32).at[0, :n_exp].set(router_b[0])
    h, idx_slab, gate_slab, pos_slab, cnt = _mix(
        x2, o_sb, o_hg, g_sb, g_hg,
        w_branch_sb[0].astype(BF16), w_branch_hgrn[0].astype(BF16), w_out[0].astype(BF16),
        ln1_g, ln1_b, rw_pad, rb_pad)

    block_e, n_used, tok_blocks, dst_blocks = _slot_tables(idx_slab, pos_slab, cnt, n, n_exp)

    wu = _regroup_up_weights(expert_w_up[0])
    bu = expert_b_up[0].reshape(n_exp, two_ff // PAIR_GROUP, LANES, 2)
    bu = bu.transpose(0, 1, 3, 2).reshape(n_exp, 1, two_ff)
    wd = expert_w_down[0].astype(BF16)
    bd = expert_b_down[0][:, None, :]
    yk = _experts(block_e, n_used, tok_blocks, dst_blocks, h, wu, bu, wd, bd)

    out = _combine(yk, h, gate_slab, ln2_g, ln2_b)
    return out.reshape(b, t, d)
```

```python
import functools
import math

import jax
import jax.numpy as jnp
import numpy as np
from jax import lax
from jax.experimental import pallas as pl
from jax.experimental.pallas import tpu as pltpu

F32 = jnp.float32
BF16 = jnp.bfloat16
I32 = jnp.int32

LANES = 128
SUBLANES = 8
PAIR_GROUP = 2 * LANES
VMEM_LIMIT_BYTES = 56 * 1024 * 1024

SB_HEAD_DIM = 64
HGRN_HEAD_DIM = 128
N_EXPERTS = 32
TOP_K = 4
SWIGLU_ALPHA = 1.702
SWIGLU_LIMIT = 7.0
DEPTH = 1
DEEPNORM_ALPHA = (2 * DEPTH) ** 0.25
LN_EPS = 1e-5
RMS_EPS = 1e-6

INPROJ_ROWS = 512
ATTN_BLOCK = 256
HGRN_CHUNK = 64
HGRN_STEP_ROWS = 256
HGRN_DIRECT = 8
MIX_ROWS = 256
EXPERT_ROWS = 256
X_SLOTS = 3
Y_SLOTS = 3
DUMP_REGIONS = 2
ISSUE_UNROLL = 8
FF_CHUNK = 512
COMBINE_ROWS = 256
NEG_BIG = -1e30
LOG2E = math.log2(math.e)


def _cparams(semantics):
    return pltpu.CompilerParams(dimension_semantics=semantics, vmem_limit_bytes=VMEM_LIMIT_BYTES)


def _split_bf16(a):
    hi = a.astype(BF16)
    lo = (a - hi.astype(F32)).astype(BF16)
    return hi, lo


def _split_trunc(a):
    bits = lax.bitcast_convert_type(a, jnp.uint32) & jnp.uint32(0xFFFF0000)
    hi = lax.bitcast_convert_type(bits, F32)
    return hi.astype(BF16), (a - hi).astype(BF16)


def _dot_nt(a, b):
    return lax.dot_general(a, b, (((1,), (1,)), ((), ())), preferred_element_type=F32)


def _dot_tn(a, b):
    return lax.dot_general(a, b, (((0,), (0,)), ((), ())), preferred_element_type=F32)


def _dot(a, b):
    return jnp.dot(a, b, preferred_element_type=F32)


def _inproj_kernel(x_ref, w_ref, *out_refs, widths):
    xb = x_ref[...].astype(BF16)
    off = 0
    for ref, width in zip(out_refs, widths):
        ref[...] = _dot(xb, w_ref[:, off:off + width]).astype(ref.dtype)
        off += width


def _inproj(x2, w_bf, widths):
    n, d = x2.shape
    tm = INPROJ_ROWS
    out_shape = [jax.ShapeDtypeStruct((n, w), BF16) for w in widths]
    out_specs = [pl.BlockSpec((tm, w), lambda i: (i, 0)) for w in widths]
    return pl.pallas_call(
        functools.partial(_inproj_kernel, widths=widths),
        grid=(n // tm,),
        in_specs=[pl.BlockSpec((tm, d), lambda i: (i, 0)),
                  pl.BlockSpec(w_bf.shape, lambda i: (0, 0))],
        out_specs=out_specs,
        out_shape=out_shape,
        compiler_params=_cparams(("arbitrary",)),
        name="inproj",
    )(x2, w_bf)


def _attn_kernel(q_ref, k_ref, v_ref, tri_ref, o_ref, *, blk, scale):
    qi = pl.program_id(2)
    q = q_ref[0].astype(F32) * scale
    lane = lax.broadcasted_iota(I32, q.shape, 1)
    first = lane < SB_HEAD_DIM
    q_heads = (jnp.where(first, q, 0.0).astype(BF16), jnp.where(first, 0.0, q).astype(BF16))
    tri2 = tri_ref[...]
    row = lax.broadcasted_iota(I32, (blk, blk), 0)
    col = lax.broadcasted_iota(I32, (blk, blk), 1)
    strictly_before = col < row
    heads = range(len(q_heads))

    def scores(kb):
        start = pl.multiple_of(kb * blk, blk)
        ks = k_ref[0, pl.ds(start, blk), :]
        return tuple(_dot_nt(qh, ks) for qh in q_heads)

    def finish(kb, zs, carry, diagonal):
        start = pl.multiple_of(kb * blk, blk)
        vs = v_ref[0, pl.ds(start, blk), :]
        ts = [jnp.log(1.0 + jnp.exp2(jnp.abs(z) * -LOG2E)) for z in zs]
        log_beta = [jnp.minimum(zs[h], 0.0) - ts[h] for h in heads]
        log_keep = [log_beta[h] - zs[h] for h in heads]
        if diagonal:
            log_keep = [jnp.where(strictly_before, lk, 0.0) for lk in log_keep]
        parts = [jnp.concatenate(_split_trunc(lk), axis=1) for lk in log_keep]
        log_after = [_dot(parts[h], tri2) + carry[h][1] for h in heads]
        ws = [jnp.exp(log_beta[h] + log_after[h]) for h in heads]
        if diagonal:
            ws = [jnp.where(strictly_before, w, 0.0) for w in ws]
        return tuple((carry[h][0] + _dot(ws[h].astype(BF16), vs),
                      carry[h][1] + jnp.sum(log_keep[h], axis=1, keepdims=True)) for h in heads)

    zero = (jnp.zeros((blk, LANES), F32), jnp.zeros((blk, 1), F32))
    carry = finish(qi, scores(qi), (zero, zero), True)

    def body(j, state):
        zs, carry = state
        kb = qi - 1 - j
        zs_next = scores(jnp.maximum(kb - 1, 0))
        return zs_next, finish(kb, zs, carry, False)

    _, carry = lax.fori_loop(0, qi, body, (scores(jnp.maximum(qi - 1, 0)), carry))
    o_ref[0] = jnp.where(first, carry[0][0], carry[1][0]).astype(o_ref.dtype)


def _attention(q, k, v):
    b, t, w = q.shape
    blk = ATTN_BLOCK
    pairs = w // LANES
    idx = np.arange(blk)
    tri = idx[:, None] > idx[None, :]
    tri = jnp.asarray(np.concatenate([tri, tri], axis=0), BF16)
    return pl.pallas_call(
        functools.partial(_attn_kernel, blk=blk, scale=SB_HEAD_DIM ** -0.5),
        grid=(b, pairs, t // blk),
        in_specs=[pl.BlockSpec((1, blk, LANES), lambda bi, hp, qi: (bi, qi, hp)),
                  pl.BlockSpec((1, t, LANES), lambda bi, hp, qi: (bi, 0, hp)),
                  pl.BlockSpec((1, t, LANES), lambda bi, hp, qi: (bi, 0, hp)),
                  pl.BlockSpec((2 * blk, blk), lambda bi, hp, qi: (0, 0))],
        out_specs=pl.BlockSpec((1, blk, LANES), lambda bi, hp, qi: (bi, qi, hp)),
        out_shape=jax.ShapeDtypeStruct((b, t, w), BF16),
        compiler_params=_cparams(("arbitrary", "arbitrary", "arbitrary")),
        name="attn",
    )(q, k, v, tri)


def _hgrn_level_sizes():
    sizes = []
    s = HGRN_CHUNK
    while s > HGRN_DIRECT:
        sizes.append(s)
        s //= 2
    return tuple(sizes)


def _hgrn_kernel(hq_ref, hf_ref, hi_ref, hg_ref, lbl_ref, ng_ref, tri_ref, mask_ref, o_ref, st_ref,
                 *, chunks):
    c = HGRN_CHUNK
    d = HGRN_DIRECT
    groups = c // d

    @pl.when(pl.program_id(2) == 0)
    def _():
        st_ref[...] = jnp.zeros_like(st_ref)

    logits = lbl_ref[...].astype(F32)
    ex = jnp.exp(logits - jnp.max(logits, axis=0, keepdims=True))
    lb = ex[0:1, :] / jnp.sum(ex, axis=0, keepdims=True)
    norm_g = ng_ref[...]
    tri = tri_ref[...]
    sub = lax.broadcasted_iota(I32, (groups, d, LANES), 1)

    for ci in range(chunks):
        rows = slice(ci * c, (ci + 1) * c)
        f = lb + (1.0 - lb) * jax.nn.sigmoid(hf_ref[0, rows, :].astype(F32))
        g = jnp.log(f)
        kk = 1.0 - f
        qq = jax.nn.silu(hq_ref[0, rows, :].astype(F32))
        vv = hi_ref[0, rows, :].astype(F32)
        vb = vv.astype(BF16)

        g_hi, g_lo = _split_bf16(g)
        bcum = _dot(tri, g_hi) + _dot(tri, g_lo)
        b_last = bcum[c - 1:c, :]

        st = st_ref[...]
        o = _dot_nt((qq * jnp.exp(bcum)).astype(BF16), st.astype(BF16))

        scores = jnp.zeros((c, c), F32)
        for li, size in enumerate(_hgrn_level_sizes()):
            half = size // 2
            ref_rows = jnp.concatenate(
                [jnp.broadcast_to(bcum[bi * size + half - 1:bi * size + half, :], (size, LANES))
                 for bi in range(c // size)], axis=0)
            q_dec = qq * jnp.exp(jnp.minimum(bcum - ref_rows, 0.0))
            k_dec = kk * jnp.exp(jnp.minimum(ref_rows - bcum, 0.0))
            sc = _dot_nt(q_dec.astype(BF16), k_dec.astype(BF16))
            scores = scores + jnp.where(mask_ref[li] > 0.0, sc, 0.0)
        o = o + _dot(scores.astype(BF16), vb)

        b3 = bcum.reshape(groups, d, LANES)
        q3 = qq.reshape(groups, d, LANES)
        k3 = kk.reshape(groups, d, LANES)
        v3 = vv.reshape(groups, d, LANES)
        od = jnp.zeros((groups, d, LANES), F32)
        for j in range(d):
            e = jnp.exp(jnp.minimum(b3 - b3[:, j:j + 1, :], 0.0))
            colj = jnp.sum(q3 * e * k3[:, j:j + 1, :], axis=-1, keepdims=True)
            colj = jnp.where(sub[:, :, 0:1] >= j, colj, 0.0)
            od = od + colj * v3[:, j:j + 1, :]
        o = o + od.reshape(c, LANES)

        k_dec = kk * jnp.exp(b_last - bcum)
        st_ref[...] = st * jnp.exp(b_last) + _dot_tn(vb, k_dec.astype(BF16))

        o = o * lax.rsqrt(jnp.mean(jnp.square(o), axis=-1, keepdims=True) + RMS_EPS)
        o = o * norm_g * jax.nn.sigmoid(hg_ref[0, rows, :].astype(F32))
        o_ref[0, rows, :] = o.astype(o_ref.dtype)


def _hgrn(hq, hf, hi, hg, lb_logits, norm_g):
    b, t, w = hq.shape
    heads = w // HGRN_HEAD_DIM
    c = HGRN_CHUNK
    rows = min(HGRN_STEP_ROWS, t)
    idx = np.arange(c)
    tri = jnp.asarray(idx[:, None] >= idx[None, :], BF16)
    masks = []
    for size in _hgrn_level_sizes():
        half = size // 2
        same = (idx[:, None] // size) == (idx[None, :] // size)
        masks.append(same & ((idx[:, None] % size) >= half) & ((idx[None, :] % size) < half))
    masks = jnp.asarray(np.stack(masks), F32)
    seq_spec = pl.BlockSpec((1, rows, LANES), lambda bi, h, ti: (bi, ti, h))
    return pl.pallas_call(
        functools.partial(_hgrn_kernel, chunks=rows // c),
        grid=(b, heads, t // rows),
        in_specs=[seq_spec, seq_spec, seq_spec, seq_spec,
                  pl.BlockSpec((lb_logits.shape[0], LANES), lambda bi, h, ti: (0, h)),
                  pl.BlockSpec((1, LANES), lambda bi, h, ti: (0, h)),
                  pl.BlockSpec((c, c), lambda bi, h, ti: (0, 0)),
                  pl.BlockSpec(masks.shape, lambda bi, h, ti: (0, 0, 0))],
        out_specs=seq_spec,
        out_shape=jax.ShapeDtypeStruct((b, t, w), BF16),
        scratch_shapes=[pltpu.VMEM((HGRN_HEAD_DIM, HGRN_HEAD_DIM), F32)],
        compiler_params=_cparams(("arbitrary", "arbitrary", "arbitrary")),
        name="hgrn",
    )(hq, hf, hi, hg, lb_logits, norm_g, tri, masks)


def _store_row_tiles(ref, row0, value):
    rows, width = value.shape
    per_row = width // LANES
    for c in range(per_row):
        ref[pl.ds(row0 * per_row + c, rows, stride=per_row), :] = value[:, c * LANES:(c + 1) * LANES]


def _load_row_tiles(ref, row0, rows, per_row):
    return [ref[pl.ds(row0 * per_row + c, rows, stride=per_row), :] for c in range(per_row)]


def _layer_norm(v, g, b):
    mu = jnp.mean(v, axis=-1, keepdims=True)
    cen = v - mu
    var = jnp.mean(jnp.square(cen), axis=-1, keepdims=True)
    return cen * lax.rsqrt(var + LN_EPS) * g + b


def _mix_kernel(x_ref, osb_ref, ohg_ref, gsb_ref, ghg_ref, wsb_ref, whg_ref, wout_ref,
                g1_ref, b1_ref, rw_ref, rb_ref, tri_ref,
                h_ref, idx_ref, gate_ref, pos_ref, cnt_ref, run_ref):
    @pl.when(pl.program_id(0) == 0)
    def _():
        run_ref[...] = jnp.zeros_like(run_ref)

    merged = (jax.nn.sigmoid(gsb_ref[...].astype(F32)) * _dot(osb_ref[...], wsb_ref[...])
              + jax.nn.sigmoid(ghg_ref[...].astype(F32)) * _dot(ohg_ref[...], whg_ref[...]))
    mix = _dot(merged.astype(BF16), wout_ref[...])
    h = _layer_norm(DEEPNORM_ALPHA * x_ref[...] + mix, g1_ref[...], b1_ref[...])
    _store_row_tiles(h_ref, 0, h)

    h_hi, h_lo = _split_bf16(h)
    h_lo2 = (h - h_hi.astype(F32) - h_lo.astype(F32)).astype(BF16)
    rw = rw_ref[...]
    rw_hi, rw_lo = _split_bf16(rw)
    rw_lo2 = (rw - rw_hi.astype(F32) - rw_lo.astype(F32)).astype(BF16)
    logits = (_dot(h_hi, rw_hi) + (_dot(h_hi, rw_lo) + _dot(h_lo, rw_hi))
              + (_dot(h_lo, rw_lo) + _dot(h_hi, rw_lo2) + _dot(h_lo2, rw_hi))) + rb_ref[...]

    tm = logits.shape[0]
    lane = lax.broadcasted_iota(I32, (tm, LANES), 1)
    remaining = logits
    vals, idxs, hots = [], [], []
    for _ in range(TOP_K):
        m = jnp.max(remaining, axis=-1, keepdims=True)
        sel = jnp.min(jnp.where(remaining == m, lane, LANES), axis=-1, keepdims=True)
        hot = lane == sel
        remaining = jnp.where(hot, NEG_BIG, remaining)
        vals.append(m)
        idxs.append(sel)
        hots.append(hot)
    exps = [jnp.exp(v - vals[0]) for v in vals]
    denom = exps[0] + exps[1] + exps[2] + exps[3]

    chosen = jnp.where(hots[0] | hots[1] | hots[2] | hots[3], 1.0, 0.0)
    before = _dot(tri_ref[...], chosen.astype(BF16)) + run_ref[0:1, :]
    idx_slab = jnp.zeros((tm, LANES), I32)
    gate_slab = jnp.zeros((tm, LANES), F32)
    pos_slab = jnp.zeros((tm, LANES), I32)
    for kk in range(TOP_K):
        rank = jnp.sum(jnp.where(hots[kk], before, 0.0), axis=-1, keepdims=True)
        here = lane == kk
        idx_slab = jnp.where(here, idxs[kk], idx_slab)
        gate_slab = jnp.where(here, exps[kk] / denom, gate_slab)
        pos_slab = jnp.where(here, rank.astype(I32), pos_slab)
    idx_ref[...] = idx_slab
    gate_ref[...] = gate_slab
    pos_ref[...] = pos_slab
    run_ref[0:1, :] = run_ref[0:1, :] + jnp.sum(chosen, axis=0, keepdims=True)
    cnt_ref[...] = run_ref[...]


def _mix(x2, o_sb, o_hg, g_sb, g_hg, wsb, whg, wout, ln_g, ln_b, rw_pad, rb_pad):
    n, d = x2.shape
    tm = MIX_ROWS
    idx = np.arange(tm)
    tri = jnp.asarray(idx[:, None] > idx[None, :], BF16)
    row = lambda w: pl.BlockSpec((tm, w), lambda i: (i, 0))
    full = lambda a: pl.BlockSpec(a.shape, lambda i: (0,) * a.ndim)
    return pl.pallas_call(
        _mix_kernel,
        grid=(n // tm,),
        in_specs=[row(d), row(o_sb.shape[1]), row(o_hg.shape[1]), row(d), row(d),
                  full(wsb), full(whg), full(wout), full(ln_g), full(ln_b),
                  full(rw_pad), full(rb_pad), full(tri)],
        out_specs=[pl.BlockSpec((tm * d // LANES, LANES), lambda i: (i, 0)),
                   row(LANES), row(LANES), row(LANES),
                   pl.BlockSpec((SUBLANES, LANES), lambda i: (0, 0))],
        out_shape=[jax.ShapeDtypeStruct((n * d // LANES, LANES), F32),
                   jax.ShapeDtypeStruct((n, LANES), I32),
                   jax.ShapeDtypeStruct((n, LANES), F32),
                   jax.ShapeDtypeStruct((n, LANES), I32),
                   jax.ShapeDtypeStruct((SUBLANES, LANES), F32)],
        scratch_shapes=[pltpu.VMEM((SUBLANES, LANES), F32)],
        compiler_params=_cparams(("arbitrary",)),
        name="mix",
    )(x2, o_sb, o_hg, g_sb, g_hg, wsb, whg, wout, ln_g, ln_b, rw_pad, rb_pad, tri)


def _pair_split_perm():
    p = np.zeros((PAIR_GROUP, PAIR_GROUP), np.float32)
    i = np.arange(LANES)
    p[2 * i, i] = 1.0
    p[2 * i + 1, LANES + i] = 1.0
    return p


def _regroup_kernel(w_ref, perm_ref, o_ref):
    perm = perm_ref[...]
    for c0 in range(0, w_ref.shape[2], PAIR_GROUP):
        cols = slice(c0, c0 + PAIR_GROUP)
        o_ref[0, :, cols] = _dot(w_ref[0, :, cols].astype(BF16), perm).astype(o_ref.dtype)


def _regroup_up_weights(w_up):
    n_exp, d, two_ff = w_up.shape
    cols = 4 * PAIR_GROUP
    perm = jnp.asarray(_pair_split_perm(), BF16)
    return pl.pallas_call(
        _regroup_kernel,
        grid=(n_exp, two_ff // cols),
        in_specs=[pl.BlockSpec((1, d, cols), lambda e, c: (e, 0, c)),
                  pl.BlockSpec(perm.shape, lambda e, c: (0, 0))],
        out_specs=pl.BlockSpec((1, d, cols), lambda e, c: (e, 0, c)),
        out_shape=jax.ShapeDtypeStruct(w_up.shape, BF16),
        compiler_params=_cparams(("arbitrary", "arbitrary")),
        name="regroup",
    )(w_up, perm)


def _expert_kernel(be_ref, nused_ref, tok_ahead_ref, tok_b0_ref, tok_b1_ref, dst_prev_ref,
                   dst_cur_ref, h_ref, wu_ref, bu_ref, wd_ref, bd_ref, yk_ref,
                   xbuf, ybuf, gsem, ssem, *, dump_row0):
    i = pl.program_id(0)
    last = pl.num_programs(0) - 1
    xs = i % X_SLOTS
    ys = i % Y_SLOTS
    prev = (i + Y_SLOTS - 1) % Y_SLOTS
    ahead = (i + 2) % X_SLOTS
    d_ff, d = wd_ref.shape[1:]
    pr = d // LANES
    tm = ybuf.shape[0] // (Y_SLOTS * pr)
    span = tm * pr

    def gather(idx_ref, r, s):
        src = h_ref.at[pl.ds(pl.multiple_of(idx_ref[0, 0, r], pr), pr)]
        dst = xbuf.at[pl.ds(pl.multiple_of(s * span + r * pr, pr), pr)]
        return pltpu.make_async_copy(src, dst, gsem.at[s])

    def scatter(idx_ref, r, s):
        src = ybuf.at[pl.ds(pl.multiple_of(s * span + r * pr, pr), pr)]
        dst = yk_ref.at[pl.ds(pl.multiple_of(idx_ref[0, 0, r], pr), pr)]
        return pltpu.make_async_copy(src, dst, ssem.at[s])

    def slot_rows(buf, s):
        return buf.at[pl.ds(pl.multiple_of(s * span, span), span)]

    def wait_gathers(s):
        pltpu.make_async_copy(h_ref.at[pl.ds(0, span)], slot_rows(xbuf, s), gsem.at[s]).wait()

    def wait_scatters(s):
        pltpu.make_async_copy(slot_rows(ybuf, s), yk_ref.at[pl.ds(0, span)], ssem.at[s]).wait()

    def gather_block(idx_ref, s):
        def one(r, _):
            gather(idx_ref, r, s).start(priority=0)
            return 0
        lax.fori_loop(0, tm, one, 0, unroll=ISSUE_UNROLL)

    def move_rows(r0, r1):
        def one(r, _):
            scatter(dst_prev_ref, r, prev).start(priority=1)
            gather(tok_ahead_ref, r, ahead).start(priority=0)
            return 0
        lax.fori_loop(r0, r1, one, 0, unroll=ISSUE_UNROLL)

    @pl.when(i == 0)
    def _():
        ybuf[...] = jnp.zeros_like(ybuf)
        for s in range(DUMP_REGIONS):
            fill = pltpu.make_async_copy(slot_rows(ybuf, s),
                                         yk_ref.at[pl.ds(dump_row0 * pr + s * span, span)],
                                         ssem.at[s])
            fill.start()
            fill.wait()
        gather_block(tok_b0_ref, 0)
        gather_block(tok_b1_ref, 1)

    wait_gathers(xs)

    @pl.when(i >= Y_SLOTS - 1)
    def _():
        wait_scatters(ys)

    n_chunks = d_ff // FF_CHUNK
    burst = tm // n_chunks

    @pl.when(i < nused_ref[0])
    def _():
        xb = jnp.concatenate(
            [c.astype(BF16) for c in _load_row_tiles(xbuf, xs * tm, tm, pr)], axis=1)
        acc = jnp.zeros((tm, d), F32) + bd_ref[0]
        for ci in range(n_chunks):
            c0 = ci * FF_CHUNK
            up = _dot(xb, wu_ref[0, :, 2 * c0:2 * (c0 + FF_CHUNK)]) + bu_ref[0, :, 2 * c0:2 * (c0 + FF_CHUNK)]
            acts = []
            for g0 in range(0, 2 * FF_CHUNK, PAIR_GROUP):
                x_glu = jnp.minimum(up[:, g0:g0 + LANES], SWIGLU_LIMIT)
                x_lin = jnp.clip(up[:, g0 + LANES:g0 + PAIR_GROUP], -SWIGLU_LIMIT, SWIGLU_LIMIT)
                acts.append(x_glu * jax.nn.sigmoid(SWIGLU_ALPHA * x_glu) * (x_lin + 1.0))
            act = jnp.concatenate(acts, axis=1)
            acc = acc + _dot(act.astype(BF16), wd_ref[0, c0:c0 + FF_CHUNK, :])
            if ci + 1 < n_chunks:
                move_rows(ci * burst, (ci + 1) * burst)
        _store_row_tiles(ybuf, ys * tm, acc)

    @pl.when(i >= nused_ref[0])
    def _():
        ybuf[pl.ds(pl.multiple_of(ys * span, span), span), :] = jnp.zeros((span, LANES), F32)
        move_rows(0, (n_chunks - 1) * burst)

    move_rows((n_chunks - 1) * burst, tm)

    @pl.when(i == last)
    def _():
        wait_scatters((i + 1) % Y_SLOTS)

        def one(r, _):
            scatter(dst_cur_ref, r, ys).start(priority=1)
            return 0
        lax.fori_loop(0, tm, one, 0, unroll=ISSUE_UNROLL)
        wait_scatters(prev)
        wait_scatters(ys)
        wait_gathers((i + 1) % X_SLOTS)
        wait_gathers(ahead)


def _experts(block_e, n_used, tok_blocks, dst_blocks, h_tiles, wu, bu, wd, bd):
    tm = EXPERT_ROWS
    n_blocks = tok_blocks.shape[0] - 2
    d_ff, d = wd.shape[1:]
    pr = d // LANES
    n = h_tiles.shape[0] // pr
    by_e3 = lambda i, be, nu: (be[i], 0, 0)
    idx_spec = lambda f: pl.BlockSpec((1, 1, tm), f, memory_space=pltpu.SMEM)
    dump_row0 = TOP_K * n
    grid_spec = pltpu.PrefetchScalarGridSpec(
        num_scalar_prefetch=2,
        grid=(n_blocks,),
        in_specs=[idx_spec(lambda i, be, nu: (i + 2, 0, 0)),
                  idx_spec(lambda i, be, nu: (0, 0, 0)),
                  idx_spec(lambda i, be, nu: (1, 0, 0)),
                  idx_spec(lambda i, be, nu: (i, 0, 0)),
                  idx_spec(lambda i, be, nu: (i + 1, 0, 0)),
                  pl.BlockSpec(memory_space=pl.ANY),
                  pl.BlockSpec((1, d, 2 * d_ff), by_e3),
                  pl.BlockSpec((1, 1, 2 * d_ff), by_e3),
                  pl.BlockSpec((1, d_ff, d), by_e3),
                  pl.BlockSpec((1, 1, d), by_e3)],
        out_specs=pl.BlockSpec(memory_space=pl.ANY),
        scratch_shapes=[pltpu.VMEM((X_SLOTS * tm * pr, LANES), F32),
                        pltpu.VMEM((Y_SLOTS * tm * pr, LANES), F32),
                        pltpu.SemaphoreType.DMA((X_SLOTS,)), pltpu.SemaphoreType.DMA((Y_SLOTS,))],
    )
    return pl.pallas_call(
        functools.partial(_expert_kernel, dump_row0=dump_row0),
        grid_spec=grid_spec,
        out_shape=jax.ShapeDtypeStruct(((dump_row0 + 2 * tm) * pr, LANES), F32),
        compiler_params=_cparams(("arbitrary",)),
        name="experts",
    )(block_e, n_used, tok_blocks * pr, tok_blocks * pr, tok_blocks * pr, dst_blocks * pr,
      dst_blocks * pr, h_tiles, wu, bu, wd, bd)


def _combine_kernel(y0_ref, y1_ref, y2_ref, y3_ref, h_ref, gate_ref, g2_ref, b2_ref, o_ref):
    rows, d = o_ref.shape
    pr = d // LANES
    gates = gate_ref[...]
    pre = [DEEPNORM_ALPHA * hc for hc in _load_row_tiles(h_ref, 0, rows, pr)]
    for kk, y_ref in enumerate((y0_ref, y1_ref, y2_ref, y3_ref)):
        gate = gates[:, kk:kk + 1]
        pre = [p + yc * gate for p, yc in zip(pre, _load_row_tiles(y_ref, 0, rows, pr))]
    o_ref[...] = _layer_norm(jnp.concatenate(pre, axis=1), g2_ref[...], b2_ref[...])


def _combine(yk_tiles, h_tiles, gates, ln_g, ln_b):
    d = ln_g.shape[1]
    pr = d // LANES
    n = h_tiles.shape[0] // pr
    rows = COMBINE_ROWS
    per_choice = n // rows
    choice = lambda kk: pl.BlockSpec((rows * pr, LANES), lambda i: (kk * per_choice + i, 0))
    return pl.pallas_call(
        _combine_kernel,
        grid=(per_choice,),
        in_specs=[choice(0), choice(1), choice(2), choice(3),
                  pl.BlockSpec((rows * pr, LANES), lambda i: (i, 0)),
                  pl.BlockSpec((rows, LANES), lambda i: (i, 0)),
                  pl.BlockSpec(ln_g.shape, lambda i: (0, 0)),
                  pl.BlockSpec(ln_b.shape, lambda i: (0, 0))],
        out_specs=pl.BlockSpec((rows, d), lambda i: (i, 0)),
        out_shape=jax.ShapeDtypeStruct((n, d), F32),
        compiler_params=_cparams(("arbitrary",)),
        name="combine",
    )(yk_tiles, yk_tiles, yk_tiles, yk_tiles, h_tiles, gates, ln_g, ln_b)


def _slot_tables(idx_slab, pos_slab, cnt, n, n_exp):
    tm = EXPERT_ROWS
    n_asg = n * TOP_K
    counts = cnt[0, :n_exp].astype(I32)
    padded = (counts + tm - 1) // tm * tm
    pad_end = jnp.cumsum(padded)
    pad_start = pad_end - padded
    start = jnp.cumsum(counts) - counts
    dest = (pad_start[idx_slab[:, :TOP_K]] + pos_slab[:, :TOP_K]).reshape(-1).astype(I32)
    n_blocks = -(-(n_asg + n_exp * (tm - 1)) // tm)
    block_row0 = jnp.arange(n_blocks, dtype=I32) * tm
    block_e = jnp.minimum(
        jnp.sum((pad_end[None, :] <= block_row0[:, None]).astype(I32), axis=1), n_exp - 1)
    n_used = (pad_end[-1:] // tm).astype(I32)

    _, order = lax.sort_key_val(dest, jnp.arange(n_asg, dtype=I32))
    slots = jnp.arange(n_blocks * tm, dtype=I32)
    e_slot = jnp.repeat(block_e, tm)
    offset = slots - pad_start[e_slot]
    valid = offset < counts[e_slot]
    asg = order[jnp.clip(start[e_slot] + offset, 0, n_asg - 1)]
    tok = jnp.where(valid, asg // TOP_K, 0)
    dst = jnp.where(valid, (asg % TOP_K) * n + tok, n_asg + slots % (2 * tm))
    tok_blocks = jnp.concatenate([tok, jnp.zeros((2 * tm,), I32)]).reshape(n_blocks + 2, 1, tm)
    before_first = n_asg + tm + jnp.arange(tm, dtype=I32)
    dst_blocks = jnp.concatenate([before_first, dst]).reshape(n_blocks + 1, 1, tm)
    return block_e, n_used, tok_blocks, dst_blocks


def kernel(x, w_in, hgrn_lb_logits, hgrn_norm_g, w_branch_sb, w_branch_hgrn, w_out, ln1_g, ln1_b,
           router_w, router_b, expert_w_up, expert_b_up, expert_w_down, expert_b_down, ln2_g, ln2_b):
    b, t, d = x.shape
    assert w_in.shape[0] == DEPTH and hgrn_lb_logits.shape[0] == DEPTH + 1
    n = b * t
    sbw = w_branch_sb.shape[1]
    hgw = w_branch_hgrn.shape[1]
    widths = (sbw, sbw, sbw, hgw, hgw, hgw, hgw, d, d)
    assert sum(widths) == w_in.shape[2]
    n_exp, _, two_ff = expert_w_up.shape[1:]
    assert n_exp == N_EXPERTS

    x2 = x.reshape(n, d)
    sq, sk, sv, hq, hf, hi, hg, g_sb, g_hg = _inproj(x2, w_in[0].astype(BF16), widths)

    seq = lambda a: a.reshape(b, t, a.shape[-1])
    o_sb = _attention(seq(sq), seq(sk), seq(sv)).reshape(n, sbw)
    o_hg = _hgrn(seq(hq), seq(hf), seq(hi), seq(hg), hgrn_lb_logits, hgrn_norm_g).reshape(n, hgw)

    rw_pad = jnp.zeros((d, LANES), F32).at[:, :n_exp].set(router_w[0])
    rb_pad = jnp.full((1, LANES), NEG_BIG, F32).at[0, :n_exp].set(router_b[0])
    h, idx_slab, gate_slab, pos_slab, cnt = _mix(
        x2, o_sb, o_hg, g_sb, g_hg,
        w_branch_sb[0].astype(BF16), w_branch_hgrn[0].astype(BF16), w_out[0].astype(BF16),
        ln1_g, ln1_b, rw_pad, rb_pad)

    block_e, n_used, tok_blocks, dst_blocks = _slot_tables(idx_slab, pos_slab, cnt, n, n_exp)

    wu = _regroup_up_weights(expert_w_up[0])
    bu = expert_b_up[0].reshape(n_exp, two_ff // PAIR_GROUP, LANES, 2)
    bu = bu.transpose(0, 1, 3, 2).reshape(n_exp, 1, two_ff)
    wd = expert_w_down[0].astype(BF16)
    bd = expert_b_down[0][:, None, :]
    yk = _experts(block_e, n_used, tok_blocks, dst_blocks, h, wu, bu, wd, bd)

    out = _combine(yk, h, gate_slab, ln2_g, ln2_b)
    return out.reshape(b, t, d)
```

```python
import functools
import math

import jax
import jax.numpy as jnp
import numpy as np
from jax import lax
from jax.experimental import pallas as pl
from jax.experimental.pallas import tpu as pltpu

F32 = jnp.float32
BF16 = jnp.bfloat16
I32 = jnp.int32

LANES = 128
SUBLANES = 8
PAIR_GROUP = 2 * LANES
VMEM_LIMIT_BYTES = 56 * 1024 * 1024

SB_HEAD_DIM = 64
HGRN_HEAD_DIM = 128
N_EXPERTS = 32
TOP_K = 4
SWIGLU_ALPHA = 1.702
SWIGLU_LIMIT = 7.0
DEPTH = 1
DEEPNORM_ALPHA = (2 * DEPTH) ** 0.25
LN_EPS = 1e-5
RMS_EPS = 1e-6

INPROJ_ROWS = 512
ATTN_BLOCK = 256
HGRN_CHUNK = 64
HGRN_STEP_ROWS = 256
HGRN_DIRECT = 8
MIX_ROWS = 256
EXPERT_ROWS = 256
X_SLOTS = 3
Y_SLOTS = 3
DUMP_REGIONS = 2
ISSUE_UNROLL = 8
FF_CHUNK = 256
COMBINE_ROWS = 256
NEG_BIG = -1e30
LOG2E = math.log2(math.e)


def _cparams(semantics):
    return pltpu.CompilerParams(dimension_semantics=semantics, vmem_limit_bytes=VMEM_LIMIT_BYTES)


def _split_bf16(a):
    hi = a.astype(BF16)
    lo = (a - hi.astype(F32)).astype(BF16)
    return hi, lo


def _split_trunc(a):
    bits = lax.bitcast_convert_type(a, jnp.uint32) & jnp.uint32(0xFFFF0000)
    hi = lax.bitcast_convert_type(bits, F32)
    return hi.astype(BF16), (a - hi).astype(BF16)


def _dot_nt(a, b):
    return lax.dot_general(a, b, (((1,), (1,)), ((), ())), preferred_element_type=F32)


def _dot_tn(a, b):
    return lax.dot_general(a, b, (((0,), (0,)), ((), ())), preferred_element_type=F32)


def _dot(a, b):
    return jnp.dot(a, b, preferred_element_type=F32)


def _inproj_kernel(x_ref, w_ref, *out_refs, widths):
    xb = x_ref[...].astype(BF16)
    off = 0
    for ref, width in zip(out_refs, widths):
        ref[...] = _dot(xb, w_ref[:, off:off + width]).astype(ref.dtype)
        off += width


def _inproj(x2, w_bf, widths):
    n, d = x2.shape
    tm = INPROJ_ROWS
    out_shape = [jax.ShapeDtypeStruct((n, w), BF16) for w in widths]
    out_specs = [pl.BlockSpec((tm, w), lambda i: (i, 0)) for w in widths]
    return pl.pallas_call(
        functools.partial(_inproj_kernel, widths=widths),
        grid=(n // tm,),
        in_specs=[pl.BlockSpec((tm, d), lambda i: (i, 0)),
                  pl.BlockSpec(w_bf.shape, lambda i: (0, 0))],
        out_specs=out_specs,
        out_shape=out_shape,
        compiler_params=_cparams(("arbitrary",)),
        name="inproj",
    )(x2, w_bf)


def _attn_kernel(q_ref, k_ref, v_ref, tri_ref, o_ref, *, blk, scale):
    qi = pl.program_id(2)
    q = q_ref[0].astype(F32) * scale
    lane = lax.broadcasted_iota(I32, q.shape, 1)
    first = lane < SB_HEAD_DIM
    q_heads = (jnp.where(first, q, 0.0).astype(BF16), jnp.where(first, 0.0, q).astype(BF16))
    tri2 = tri_ref[...]
    row = lax.broadcasted_iota(I32, (blk, blk), 0)
    col = lax.broadcasted_iota(I32, (blk, blk), 1)
    strictly_before = col < row
    heads = range(len(q_heads))

    def scores(kb):
        start = pl.multiple_of(kb * blk, blk)
        ks = k_ref[0, pl.ds(start, blk), :]
        return tuple(_dot_nt(qh, ks) for qh in q_heads)

    def finish(kb, zs, carry, diagonal):
        start = pl.multiple_of(kb * blk, blk)
        vs = v_ref[0, pl.ds(start, blk), :]
        ts = [jnp.log(1.0 + jnp.exp2(jnp.abs(z) * -LOG2E)) for z in zs]
        log_beta = [jnp.minimum(zs[h], 0.0) - ts[h] for h in heads]
        log_keep = [log_beta[h] - zs[h] for h in heads]
        if diagonal:
            log_keep = [jnp.where(strictly_before, lk, 0.0) for lk in log_keep]
        parts = [jnp.concatenate(_split_trunc(lk), axis=1) for lk in log_keep]
        log_after = [_dot(parts[h], tri2) + carry[h][1] for h in heads]
        ws = [jnp.exp(log_beta[h] + log_after[h]) for h in heads]
        if diagonal:
            ws = [jnp.where(strictly_before, w, 0.0) for w in ws]
        return tuple((carry[h][0] + _dot(ws[h].astype(BF16), vs),
                      carry[h][1] + jnp.sum(log_keep[h], axis=1, keepdims=True)) for h in heads)

    zero = (jnp.zeros((blk, LANES), F32), jnp.zeros((blk, 1), F32))
    carry = finish(qi, scores(qi), (zero, zero), True)

    def body(j, state):
        zs, carry = state
        kb = qi - 1 - j
        zs_next = scores(jnp.maximum(kb - 1, 0))
        return zs_next, finish(kb, zs, carry, False)

    _, carry = lax.fori_loop(0, qi, body, (scores(jnp.maximum(qi - 1, 0)), carry))
    o_ref[0] = jnp.where(first, carry[0][0], carry[1][0]).astype(o_ref.dtype)


def _attention(q, k, v):
    b, t, w = q.shape
    blk = ATTN_BLOCK
    pairs = w // LANES
    idx = np.arange(blk)
    tri = idx[:, None] > idx[None, :]
    tri = jnp.asarray(np.concatenate([tri, tri], axis=0), BF16)
    return pl.pallas_call(
        functools.partial(_attn_kernel, blk=blk, scale=SB_HEAD_DIM ** -0.5),
        grid=(b, pairs, t // blk),
        in_specs=[pl.BlockSpec((1, blk, LANES), lambda bi, hp, qi: (bi, qi, hp)),
                  pl.BlockSpec((1, t, LANES), lambda bi, hp, qi: (bi, 0, hp)),
                  pl.BlockSpec((1, t, LANES), lambda bi, hp, qi: (bi, 0, hp)),
                  pl.BlockSpec((2 * blk, blk), lambda bi, hp, qi: (0, 0))],
        out_specs=pl.BlockSpec((1, blk, LANES), lambda bi, hp, qi: (bi, qi, hp)),
        out_shape=jax.ShapeDtypeStruct((b, t, w), BF16),
        compiler_params=_cparams(("arbitrary", "arbitrary", "arbitrary")),
        name="attn",
    )(q, k, v, tri)


def _hgrn_level_sizes():
    sizes = []
    s = HGRN_CHUNK
    while s > HGRN_DIRECT:
        sizes.append(s)
        s //= 2
    return tuple(sizes)


def _hgrn_kernel(hq_ref, hf_ref, hi_ref, hg_ref, lbl_ref, ng_ref, tri_ref, mask_ref, o_ref, st_ref,
                 *, chunks):
    c = HGRN_CHUNK
    d = HGRN_DIRECT
    groups = c // d

    @pl.when(pl.program_id(2) == 0)
    def _():
        st_ref[...] = jnp.zeros_like(st_ref)

    logits = lbl_ref[...].astype(F32)
    ex = jnp.exp(logits - jnp.max(logits, axis=0, keepdims=True))
    lb = ex[0:1, :] / jnp.sum(ex, axis=0, keepdims=True)
    norm_g = ng_ref[...]
    tri = tri_ref[...]
    sub = lax.broadcasted_iota(I32, (groups, d, LANES), 1)

    for ci in range(chunks):
        rows = slice(ci * c, (ci + 1) * c)
        f = lb + (1.0 - lb) * jax.nn.sigmoid(hf_ref[0, rows, :].astype(F32))
        g = jnp.log(f)
        kk = 1.0 - f
        qq = jax.nn.silu(hq_ref[0, rows, :].astype(F32))
        vv = hi_ref[0, rows, :].astype(F32)
        vb = vv.astype(BF16)

        g_hi, g_lo = _split_bf16(g)
        bcum = _dot(tri, g_hi) + _dot(tri, g_lo)
        b_last = bcum[c - 1:c, :]

        st = st_ref[...]
        o = _dot_nt((qq * jnp.exp(bcum)).astype(BF16), st.astype(BF16))

        scores = jnp.zeros((c, c), F32)
        for li, size in enumerate(_hgrn_level_sizes()):
            half = size // 2
            ref_rows = jnp.concatenate(
                [jnp.broadcast_to(bcum[bi * size + half - 1:bi * size + half, :], (size, LANES))
                 for bi in range(c // size)], axis=0)
            q_dec = qq * jnp.exp(jnp.minimum(bcum - ref_rows, 0.0))
            k_dec = kk * jnp.exp(jnp.minimum(ref_rows - bcum, 0.0))
            sc = _dot_nt(q_dec.astype(BF16), k_dec.astype(BF16))
            scores = scores + jnp.where(mask_ref[li] > 0.0, sc, 0.0)
        o = o + _dot(scores.astype(BF16), vb)

        b3 = bcum.reshape(groups, d, LANES)
        q3 = qq.reshape(groups, d, LANES)
        k3 = kk.reshape(groups, d, LANES)
        v3 = vv.reshape(groups, d, LANES)
        od = jnp.zeros((groups, d, LANES), F32)
        for j in range(d):
            e = jnp.exp(jnp.minimum(b3 - b3[:, j:j + 1, :], 0.0))
            colj = jnp.sum(q3 * e * k3[:, j:j + 1, :], axis=-1, keepdims=True)
            colj = jnp.where(sub[:, :, 0:1] >= j, colj, 0.0)
            od = od + colj * v3[:, j:j + 1, :]
        o = o + od.reshape(c, LANES)

        k_dec = kk * jnp.exp(b_last - bcum)
        st_ref[...] = st * jnp.exp(b_last) + _dot_tn(vb, k_dec.astype(BF16))

        o = o * lax.rsqrt(jnp.mean(jnp.square(o), axis=-1, keepdims=True) + RMS_EPS)
        o = o * norm_g * jax.nn.sigmoid(hg_ref[0, rows, :].astype(F32))
        o_ref[0, rows, :] = o.astype(o_ref.dtype)


def _hgrn(hq, hf, hi, hg, lb_logits, norm_g):
    b, t, w = hq.shape
    heads = w // HGRN_HEAD_DIM
    c = HGRN_CHUNK
    rows = min(HGRN_STEP_ROWS, t)
    idx = np.arange(c)
    tri = jnp.asarray(idx[:, None] >= idx[None, :], BF16)
    masks = []
    for size in _hgrn_level_sizes():
        half = size // 2
        same = (idx[:, None] // size) == (idx[None, :] // size)
        masks.append(same & ((idx[:, None] % size) >= half) & ((idx[None, :] % size) < half))
    masks = jnp.asarray(np.stack(masks), F32)
    seq_spec = pl.BlockSpec((1, rows, LANES), lambda bi, h, ti: (bi, ti, h))
    return pl.pallas_call(
        functools.partial(_hgrn_kernel, chunks=rows // c),
        grid=(b, heads, t // rows),
        in_specs=[seq_spec, seq_spec, seq_spec, seq_spec,
                  pl.BlockSpec((lb_logits.shape[0], LANES), lambda bi, h, ti: (0, h)),
                  pl.BlockSpec((1, LANES), lambda bi, h, ti: (0, h)),
                  pl.BlockSpec((c, c), lambda bi, h, ti: (0, 0)),
                  pl.BlockSpec(masks.shape, lambda bi, h, ti: (0, 0, 0))],
        out_specs=seq_spec,
        out_shape=jax.ShapeDtypeStruct((b, t, w), BF16),
        scratch_shapes=[pltpu.VMEM((HGRN_HEAD_DIM, HGRN_HEAD_DIM), F32)],
        compiler_params=_cparams(("arbitrary", "arbitrary", "arbitrary")),
        name="hgrn",
    )(hq, hf, hi, hg, lb_logits, norm_g, tri, masks)


def _store_row_tiles(ref, row0, value):
    rows, width = value.shape
    per_row = width // LANES
    for c in range(per_row):
        ref[pl.ds(row0 * per_row + c, rows, stride=per_row), :] = value[:, c * LANES:(c + 1) * LANES]


def _load_row_tiles(ref, row0, rows, per_row):
    return [ref[pl.ds(row0 * per_row + c, rows, stride=per_row), :] for c in range(per_row)]


def _layer_norm(v, g, b):
    mu = jnp.mean(v, axis=-1, keepdims=True)
    cen = v - mu
    var = jnp.mean(jnp.square(cen), axis=-1, keepdims=True)
    return cen * lax.rsqrt(var + LN_EPS) * g + b


def _mix_kernel(x_ref, osb_ref, ohg_ref, gsb_ref, ghg_ref, wsb_ref, whg_ref, wout_ref,
                g1_ref, b1_ref, rw_ref, rb_ref, tri_ref,
                h_ref, idx_ref, gate_ref, pos_ref, cnt_ref, run_ref):
    @pl.when(pl.program_id(0) == 0)
    def _():
        run_ref[...] = jnp.zeros_like(run_ref)

    merged = (jax.nn.sigmoid(gsb_ref[...].astype(F32)) * _dot(osb_ref[...], wsb_ref[...])
              + jax.nn.sigmoid(ghg_ref[...].astype(F32)) * _dot(ohg_ref[...], whg_ref[...]))
    mix = _dot(merged.astype(BF16), wout_ref[...])
    h = _layer_norm(DEEPNORM_ALPHA * x_ref[...] + mix, g1_ref[...], b1_ref[...])
    _store_row_tiles(h_ref, 0, h)

    h_hi, h_lo = _split_bf16(h)
    h_lo2 = (h - h_hi.astype(F32) - h_lo.astype(F32)).astype(BF16)
    rw = rw_ref[...]
    rw_hi, rw_lo = _split_bf16(rw)
    rw_lo2 = (rw - rw_hi.astype(F32) - rw_lo.astype(F32)).astype(BF16)
    logits = (_dot(h_hi, rw_hi) + (_dot(h_hi, rw_lo) + _dot(h_lo, rw_hi))
              + (_dot(h_lo, rw_lo) + _dot(h_hi, rw_lo2) + _dot(h_lo2, rw_hi))) + rb_ref[...]

    tm = logits.shape[0]
    lane = lax.broadcasted_iota(I32, (tm, LANES), 1)
    remaining = logits
    vals, idxs, hots = [], [], []
    for _ in range(TOP_K):
        m = jnp.max(remaining, axis=-1, keepdims=True)
        sel = jnp.min(jnp.where(remaining == m, lane, LANES), axis=-1, keepdims=True)
        hot = lane == sel
        remaining = jnp.where(hot, NEG_BIG, remaining)
        vals.append(m)
        idxs.append(sel)
        hots.append(hot)
    exps = [jnp.exp(v - vals[0]) for v in vals]
    denom = exps[0] + exps[1] + exps[2] + exps[3]

    chosen = jnp.where(hots[0] | hots[1] | hots[2] | hots[3], 1.0, 0.0)
    before = _dot(tri_ref[...], chosen.astype(BF16)) + run_ref[0:1, :]
    idx_slab = jnp.zeros((tm, LANES), I32)
    gate_slab = jnp.zeros((tm, LANES), F32)
    pos_slab = jnp.zeros((tm, LANES), I32)
    for kk in range(TOP_K):
        rank = jnp.sum(jnp.where(hots[kk], before, 0.0), axis=-1, keepdims=True)
        here = lane == kk
        idx_slab = jnp.where(here, idxs[kk], idx_slab)
        gate_slab = jnp.where(here, exps[kk] / denom, gate_slab)
        pos_slab = jnp.where(here, rank.astype(I32), pos_slab)
    idx_ref[...] = idx_slab
    gate_ref[...] = gate_slab
    pos_ref[...] = pos_slab
    run_ref[0:1, :] = run_ref[0:1, :] + jnp.sum(chosen, axis=0, keepdims=True)
    cnt_ref[...] = run_ref[...]


def _mix(x2, o_sb, o_hg, g_sb, g_hg, wsb, whg, wout, ln_g, ln_b, rw_pad, rb_pad):
    n, d = x2.shape
    tm = MIX_ROWS
    idx = np.arange(tm)
    tri = jnp.asarray(idx[:, None] > idx[None, :], BF16)
    row = lambda w: pl.BlockSpec((tm, w), lambda i: (i, 0))
    full = lambda a: pl.BlockSpec(a.shape, lambda i: (0,) * a.ndim)
    return pl.pallas_call(
        _mix_kernel,
        grid=(n // tm,),
        in_specs=[row(d), row(o_sb.shape[1]), row(o_hg.shape[1]), row(d), row(d),
                  full(wsb), full(whg), full(wout), full(ln_g), full(ln_b),
                  full(rw_pad), full(rb_pad), full(tri)],
        out_specs=[pl.BlockSpec((tm * d // LANES, LANES), lambda i: (i, 0)),
                   row(LANES), row(LANES), row(LANES),
                   pl.BlockSpec((SUBLANES, LANES), lambda i: (0, 0))],
        out_shape=[jax.ShapeDtypeStruct((n * d // LANES, LANES), F32),
                   jax.ShapeDtypeStruct((n, LANES), I32),
                   jax.ShapeDtypeStruct((n, LANES), F32),
                   jax.ShapeDtypeStruct((n, LANES), I32),
                   jax.ShapeDtypeStruct((SUBLANES, LANES), F32)],
        scratch_shapes=[pltpu.VMEM((SUBLANES, LANES), F32)],
        compiler_params=_cparams(("arbitrary",)),
        name="mix",
    )(x2, o_sb, o_hg, g_sb, g_hg, wsb, whg, wout, ln_g, ln_b, rw_pad, rb_pad, tri)


def _pair_split_perm():
    p = np.zeros((PAIR_GROUP, PAIR_GROUP), np.float32)
    i = np.arange(LANES)
    p[2 * i, i] = 1.0
    p[2 * i + 1, LANES + i] = 1.0
    return p


def _regroup_kernel(w_ref, perm_ref, o_ref):
    perm = perm_ref[...]
    for c0 in range(0, w_ref.shape[2], PAIR_GROUP):
        cols = slice(c0, c0 + PAIR_GROUP)
        o_ref[0, :, cols] = _dot(w_ref[0, :, cols].astype(BF16), perm).astype(o_ref.dtype)


def _regroup_up_weights(w_up):
    n_exp, d, two_ff = w_up.shape
    cols = 4 * PAIR_GROUP
    perm = jnp.asarray(_pair_split_perm(), BF16)
    return pl.pallas_call(
        _regroup_kernel,
        grid=(n_exp, two_ff // cols),
        in_specs=[pl.BlockSpec((1, d, cols), lambda e, c: (e, 0, c)),
                  pl.BlockSpec(perm.shape, lambda e, c: (0, 0))],
        out_specs=pl.BlockSpec((1, d, cols), lambda e, c: (e, 0, c)),
        out_shape=jax.ShapeDtypeStruct(w_up.shape, BF16),
        compiler_params=_cparams(("arbitrary", "arbitrary")),
        name="regroup",
    )(w_up, perm)


def _expert_kernel(be_ref, nused_ref, tok_ahead_ref, tok_b0_ref, tok_b1_ref, dst_prev_ref,
                   dst_cur_ref, h_ref, wu_ref, bu_ref, wd_ref, bd_ref, yk_ref,
                   xbuf, ybuf, gsem, ssem, *, dump_row0):
    i = pl.program_id(0)
    last = pl.num_programs(0) - 1
    xs = i % X_SLOTS
    ys = i % Y_SLOTS
    prev = (i + Y_SLOTS - 1) % Y_SLOTS
    ahead = (i + 2) % X_SLOTS
    d_ff, d = wd_ref.shape[1:]
    pr = d // LANES
    tm = ybuf.shape[0] // (Y_SLOTS * pr)
    span = tm * pr

    def gather(idx_ref, r, s):
        src = h_ref.at[pl.ds(pl.multiple_of(idx_ref[0, 0, r], pr), pr)]
        dst = xbuf.at[pl.ds(pl.multiple_of(s * span + r * pr, pr), pr)]
        return pltpu.make_async_copy(src, dst, gsem.at[s])

    def scatter(idx_ref, r, s):
        src = ybuf.at[pl.ds(pl.multiple_of(s * span + r * pr, pr), pr)]
        dst = yk_ref.at[pl.ds(pl.multiple_of(idx_ref[0, 0, r], pr), pr)]
        return pltpu.make_async_copy(src, dst, ssem.at[s])

    def slot_rows(buf, s):
        return buf.at[pl.ds(pl.multiple_of(s * span, span), span)]

    def wait_gathers(s):
        pltpu.make_async_copy(h_ref.at[pl.ds(0, span)], slot_rows(xbuf, s), gsem.at[s]).wait()

    def wait_scatters(s):
        pltpu.make_async_copy(slot_rows(ybuf, s), yk_ref.at[pl.ds(0, span)], ssem.at[s]).wait()

    def gather_block(idx_ref, s):
        def one(r, _):
            gather(idx_ref, r, s).start(priority=0)
            return 0
        lax.fori_loop(0, tm, one, 0, unroll=ISSUE_UNROLL)

    def move_rows(r0, r1):
        def one(r, _):
            scatter(dst_prev_ref, r, prev).start(priority=1)
            gather(tok_ahead_ref, r, ahead).start(priority=0)
            return 0
        lax.fori_loop(r0, r1, one, 0, unroll=ISSUE_UNROLL)

    @pl.when(i == 0)
    def _():
        ybuf[...] = jnp.zeros_like(ybuf)
        for s in range(DUMP_REGIONS):
            fill = pltpu.make_async_copy(slot_rows(ybuf, s),
                                         yk_ref.at[pl.ds(dump_row0 * pr + s * span, span)],
                                         ssem.at[s])
            fill.start()
            fill.wait()
        gather_block(tok_b0_ref, 0)
        gather_block(tok_b1_ref, 1)

    wait_gathers(xs)

    @pl.when(i >= Y_SLOTS - 1)
    def _():
        wait_scatters(ys)

    n_chunks = d_ff // FF_CHUNK
    burst = tm // n_chunks

    @pl.when(i < nused_ref[0])
    def _():
        xb = jnp.concatenate(
            [c.astype(BF16) for c in _load_row_tiles(xbuf, xs * tm, tm, pr)], axis=1)
        acc = jnp.zeros((tm, d), F32) + bd_ref[0]
        for ci in range(n_chunks):
            c0 = ci * FF_CHUNK
            up = _dot(xb, wu_ref[0, :, 2 * c0:2 * (c0 + FF_CHUNK)]) + bu_ref[0, :, 2 * c0:2 * (c0 + FF_CHUNK)]
            acts = []
            for g0 in range(0, 2 * FF_CHUNK, PAIR_GROUP):
                x_glu = jnp.minimum(up[:, g0:g0 + LANES], SWIGLU_LIMIT)
                x_lin = jnp.clip(up[:, g0 + LANES:g0 + PAIR_GROUP], -SWIGLU_LIMIT, SWIGLU_LIMIT)
                acts.append(x_glu * jax.nn.sigmoid(SWIGLU_ALPHA * x_glu) * (x_lin + 1.0))
            act = jnp.concatenate(acts, axis=1)
            acc = acc + _dot(act.astype(BF16), wd_ref[0, c0:c0 + FF_CHUNK, :])
            if ci + 1 < n_chunks:
                move_rows(ci * burst, (ci + 1) * burst)
        _store_row_tiles(ybuf, ys * tm, acc)

    @pl.when(i >= nused_ref[0])
    def _():
        ybuf[pl.ds(pl.multiple_of(ys * span, span), span), :] = jnp.zeros((span, LANES), F32)
        move_rows(0, (n_chunks - 1) * burst)

    move_rows((n_chunks - 1) * burst, tm)

    @pl.when(i == last)
    def _():
        wait_scatters((i + 1) % Y_SLOTS)

        def one(r, _):
            scatter(dst_cur_ref, r, ys).start(priority=1)
            return 0
        lax.fori_loop(0, tm, one, 0, unroll=ISSUE_UNROLL)
        wait_scatters(prev)
        wait_scatters(ys)
        wait_gathers((i + 1) % X_SLOTS)
        wait_gathers(ahead)


def _experts(block_e, n_used, tok_blocks, dst_blocks, h_tiles, wu, bu, wd, bd):
    tm = EXPERT_ROWS
    n_blocks = tok_blocks.shape[0] - 2
    d_ff, d = wd.shape[1:]
    pr = d // LANES
    n = h_tiles.shape[0] // pr
    by_e3 = lambda i, be, nu: (be[i], 0, 0)
    idx_spec = lambda f: pl.BlockSpec((1, 1, tm), f, memory_space=pltpu.SMEM)
    dump_row0 = TOP_K * n
    grid_spec = pltpu.PrefetchScalarGridSpec(
        num_scalar_prefetch=2,
        grid=(n_blocks,),
        in_specs=[idx_spec(lambda i, be, nu: (i + 2, 0, 0)),
                  idx_spec(lambda i, be, nu: (0, 0, 0)),
                  idx_spec(lambda i, be, nu: (1, 0, 0)),
                  idx_spec(lambda i, be, nu: (i, 0, 0)),
                  idx_spec(lambda i, be, nu: (i + 1, 0, 0)),
                  pl.BlockSpec(memory_space=pl.ANY),
                  pl.BlockSpec((1, d, 2 * d_ff), by_e3),
                  pl.BlockSpec((1, 1, 2 * d_ff), by_e3),
                  pl.BlockSpec((1, d_ff, d), by_e3),
                  pl.BlockSpec((1, 1, d), by_e3)],
        out_specs=pl.BlockSpec(memory_space=pl.ANY),
        scratch_shapes=[pltpu.VMEM((X_SLOTS * tm * pr, LANES), F32),
                        pltpu.VMEM((Y_SLOTS * tm * pr, LANES), F32),
                        pltpu.SemaphoreType.DMA((X_SLOTS,)), pltpu.SemaphoreType.DMA((Y_SLOTS,))],
    )
    return pl.pallas_call(
        functools.partial(_expert_kernel, dump_row0=dump_row0),
        grid_spec=grid_spec,
        out_shape=jax.ShapeDtypeStruct(((dump_row0 + 2 * tm) * pr, LANES), F32),
        compiler_params=_cparams(("arbitrary",)),
        name="experts",
    )(block_e, n_used, tok_blocks * pr, tok_blocks * pr, tok_blocks * pr, dst_blocks * pr,
      dst_blocks * pr, h_tiles, wu, bu, wd, bd)


def _combine_kernel(y0_ref, y1_ref, y2_ref, y3_ref, h_ref, gate_ref, g2_ref, b2_ref, o_ref):
    rows, d = o_ref.shape
    pr = d // LANES
    gates = gate_ref[...]
    pre = [DEEPNORM_ALPHA * hc for hc in _load_row_tiles(h_ref, 0, rows, pr)]
    for kk, y_ref in enumerate((y0_ref, y1_ref, y2_ref, y3_ref)):
        gate = gates[:, kk:kk + 1]
        pre = [p + yc * gate for p, yc in zip(pre, _load_row_tiles(y_ref, 0, rows, pr))]
    o_ref[...] = _layer_norm(jnp.concatenate(pre, axis=1), g2_ref[...], b2_ref[...])


def _combine(yk_tiles, h_tiles, gates, ln_g, ln_b):
    d = ln_g.shape[1]
    pr = d // LANES
    n = h_tiles.shape[0] // pr
    rows = COMBINE_ROWS
    per_choice = n // rows
    choice = lambda kk: pl.BlockSpec((rows * pr, LANES), lambda i: (kk * per_choice + i, 0))
    return pl.pallas_call(
        _combine_kernel,
        grid=(per_choice,),
        in_specs=[choice(0), choice(1), choice(2), choice(3),
                  pl.BlockSpec((rows * pr, LANES), lambda i: (i, 0)),
                  pl.BlockSpec((rows, LANES), lambda i: (i, 0)),
                  pl.BlockSpec(ln_g.shape, lambda i: (0, 0)),
                  pl.BlockSpec(ln_b.shape, lambda i: (0, 0))],
        out_specs=pl.BlockSpec((rows, d), lambda i: (i, 0)),
        out_shape=jax.ShapeDtypeStruct((n, d), F32),
        compiler_params=_cparams(("arbitrary",)),
        name="combine",
    )(yk_tiles, yk_tiles, yk_tiles, yk_tiles, h_tiles, gates, ln_g, ln_b)


def _slot_tables(idx_slab, pos_slab, cnt, n, n_exp):
    tm = EXPERT_ROWS
    n_asg = n * TOP_K
    counts = cnt[0, :n_exp].astype(I32)
    padded = (counts + tm - 1) // tm * tm
    pad_end = jnp.cumsum(padded)
    pad_start = pad_end - padded
    start = jnp.cumsum(counts) - counts
    dest = (pad_start[idx_slab[:, :TOP_K]] + pos_slab[:, :TOP_K]).reshape(-1).astype(I32)
    n_blocks = -(-(n_asg + n_exp * (tm - 1)) // tm)
    block_row0 = jnp.arange(n_blocks, dtype=I32) * tm
    block_e = jnp.minimum(
        jnp.sum((pad_end[None, :] <= block_row0[:, None]).astype(I32), axis=1), n_exp - 1)
    n_used = (pad_end[-1:] // tm).astype(I32)

    _, order = lax.sort_key_val(dest, jnp.arange(n_asg, dtype=I32))
    slots = jnp.arange(n_blocks * tm, dtype=I32)
    e_slot = jnp.repeat(block_e, tm)
    offset = slots - pad_start[e_slot]
    valid = offset < counts[e_slot]
    asg = order[jnp.clip(start[e_slot] + offset, 0, n_asg - 1)]
    tok = jnp.where(valid, asg // TOP_K, 0)
    dst = jnp.where(valid, (asg % TOP_K) * n + tok, n_asg + slots % (2 * tm))
    tok_blocks = jnp.concatenate([tok, jnp.zeros((2 * tm,), I32)]).reshape(n_blocks + 2, 1, tm)
    before_first = n_asg + tm + jnp.arange(tm, dtype=I32)
    dst_blocks = jnp.concatenate([before_first, dst]).reshape(n_blocks + 1, 1, tm)
    return block_e, n_used, tok_blocks, dst_blocks


def kernel(x, w_in, hgrn_lb_logits, hgrn_norm_g, w_branch_sb, w_branch_hgrn, w_out, ln1_g, ln1_b,
           router_w, router_b, expert_w_up, expert_b_up, expert_w_down, expert_b_down, ln2_g, ln2_b):
    b, t, d = x.shape
    assert w_in.shape[0] == DEPTH and hgrn_lb_logits.shape[0] == DEPTH + 1
    n = b * t
    sbw = w_branch_sb.shape[1]
    hgw = w_branch_hgrn.shape[1]
    widths = (sbw, sbw, sbw, hgw, hgw, hgw, hgw, d, d)
    assert sum(widths) == w_in.shape[2]
    n_exp, _, two_ff = expert_w_up.shape[1:]
    assert n_exp == N_EXPERTS

    x2 = x.reshape(n, d)
    sq, sk, sv, hq, hf, hi, hg, g_sb, g_hg = _inproj(x2, w_in[0].astype(BF16), widths)

    seq = lambda a: a.reshape(b, t, a.shape[-1])
    o_sb = _attention(seq(sq), seq(sk), seq(sv)).reshape(n, sbw)
    o_hg = _hgrn(seq(hq), seq(hf), seq(hi), seq(hg), hgrn_lb_logits, hgrn_norm_g).reshape(n, hgw)

    rw_pad = jnp.zeros((d, LANES), F32).at[:, :n_exp].set(router_w[0])
    rb_pad = jnp.full((1, LANES), NEG_BIG, F32).at[0, :n_exp].set(router_b[0])
    h, idx_slab, gate_slab, pos_slab, cnt = _mix(
        x2, o_sb, o_hg, g_sb, g_hg,
        w_branch_sb[0].astype(BF16), w_branch_hgrn[0].astype(BF16), w_out[0].astype(BF16),
        ln1_g, ln1_b, rw_pad, rb_pad)

    block_e, n_used, tok_blocks, dst_blocks = _slot_tables(idx_slab, pos_slab, cnt, n, n_exp)

    wu = _regroup_up_weights(expert_w_up[0])
    bu = expert_b_up[0].reshape(n_exp, two_ff // PAIR_GROUP, LANES, 2)
    bu = bu.transpose(0, 1, 3, 2).reshape(n_exp, 1, two_ff)
    wd = expert_w_down[0].astype(BF16)
    bd = expert_b_down[0][:, None, :]
    yk = _experts(block_e, n_used, tok_blocks, dst_blocks, h, wu, bu, wd, bd)

    out = _combine(yk, h, gate_slab, ln2_g, ln2_b)
    return out.reshape(b, t, d)
```
